```python
import math
import jax, jax.numpy as jnp
from jax import lax
import numpy as np

D_MODEL = 1024
BATCH = 16
SEQ = 4096
DEPTH = 4
DEC_BATCH = 8
DEC_SEQ = 4096
PAST_LEN = 128

MLA_HEADS = 8
MLA_NOPE = 64
MLA_ROPE = 32
MLA_V = 64
Q_LORA = 256
KV_LORA = 128
ROPE_THETA = 10000.0
Q_BLOCK = 128
GDN_HEADS = 8
GDN_DK = 64
GDN_DV = 64
CONV_WIDTH = 3
CHUNK = 64
D_FF = -(-8 * D_MODEL // (3 * 256)) * 256
NORM_EPS = 1e-6

MIX_WIDTH = MLA_HEADS * MLA_V + GDN_HEADS * GDN_DV
QKV_CH = GDN_HEADS * (2 * GDN_DK + GDN_DV)
IN_SPLITS = (Q_LORA, KV_LORA, MLA_ROPE, QKV_CH, GDN_HEADS * GDN_DV, 2 * GDN_HEADS, 2 * GDN_HEADS)
IN_COLS = sum(IN_SPLITS)
IN_SPLIT_POINTS = tuple(int(v) for v in np.cumsum(IN_SPLITS)[:-1])

kernel_name = "hymba_mla_gdn_bidir_encoder"


def rms_norm(x, g):
    xf = x.astype(jnp.float32)
    y = xf * lax.rsqrt(jnp.mean(xf * xf, axis=-1, keepdims=True) + NORM_EPS)
    return (y * g.astype(jnp.float32)).astype(x.dtype)


def l2_normalize(x):
    xf = x.astype(jnp.float32)
    return xf * lax.rsqrt(jnp.sum(xf * xf, axis=-1, keepdims=True) + NORM_EPS)


def rope_tables(seq):
    pos = jnp.arange(seq, dtype=jnp.float32)
    inv = ROPE_THETA ** (-jnp.arange(0, MLA_ROPE, 2, dtype=jnp.float32) / MLA_ROPE)
    ang = pos[:, None] * inv[None, :]
    return jnp.cos(ang), jnp.sin(ang)


def apply_rope(x, cos, sin):
    c = cos[None, :, None, :].astype(x.dtype)
    s = sin[None, :, None, :].astype(x.dtype)
    x1, x2 = x[..., : MLA_ROPE // 2], x[..., MLA_ROPE // 2:]
    return jnp.concatenate([x1 * c - x2 * s, x2 * c + x1 * s], axis=-1)


def mla_mixer(q_lat, kv_lat, k_rope, q_norm, w_q_up, kv_norm, w_kv_up, cos, sin):
    B, S, _ = q_lat.shape
    q = (rms_norm(q_lat, q_norm) @ w_q_up).reshape(B, S, MLA_HEADS, MLA_NOPE + MLA_ROPE)
    q = jnp.concatenate([q[..., :MLA_NOPE], apply_rope(q[..., MLA_NOPE:], cos, sin)], axis=-1)
    kv = (rms_norm(kv_lat, kv_norm) @ w_kv_up).reshape(B, S, MLA_HEADS, MLA_NOPE + MLA_V)
    k_nope, v = kv[..., :MLA_NOPE], kv[..., MLA_NOPE:]
    k_pe = apply_rope(k_rope[:, :, None, :], cos, sin)
    k = jnp.concatenate([k_nope, jnp.broadcast_to(k_pe, (B, S, MLA_HEADS, MLA_ROPE))], axis=-1)
    scale = (MLA_NOPE + MLA_ROPE) ** -0.5
    nb = S // Q_BLOCK
    qb = jnp.moveaxis(q.reshape(B, nb, Q_BLOCK, MLA_HEADS, MLA_NOPE + MLA_ROPE), 1, 0)

    def attend(q_blk):
        s = jnp.einsum('bqhd,bkhd->bhqk', q_blk, k).astype(jnp.float32) * scale
        p = jax.nn.softmax(s, axis=-1).astype(v.dtype)
        return jnp.einsum('bhqk,bkhv->bqhv', p, v)

    o = lax.map(attend, qb)
    return jnp.moveaxis(o, 0, 1).reshape(B, S, MLA_HEADS * MLA_V)


def depthwise_conv_centred(x, w):
    C = x.shape[-1]
    return lax.conv_general_dilated(
        x, w[:, None, :].astype(x.dtype), window_strides=(1,),
        padding=[(CONV_WIDTH // 2, CONV_WIDTH // 2)],
        dimension_numbers=('NWC', 'WIO', 'NWC'), feature_group_count=C)


def gated_delta_chunked(q, k, v, g, beta):
    B, S, H, DK = q.shape
    DV = v.shape[-1]
    n = S // CHUNK

    def to_chunks(t):
        t = jnp.moveaxis(t.astype(jnp.float32), 2, 1)
        return t.reshape((B, H, n, CHUNK) + t.shape[3:])

    q = to_chunks(q) * (DK ** -0.5)
    k = to_chunks(k)
    v = to_chunks(v)
    g = to_chunks(g)
    beta = to_chunks(beta)
    gc = jnp.cumsum(g, axis=-1)
    idx = jnp.arange(CHUNK)
    incl = idx[:, None] >= idx[None, :]
    strict = idx[:, None] > idx[None, :]
    decay = jnp.exp(jnp.where(incl, gc[..., :, None] - gc[..., None, :], -jnp.inf))
    kb = k * beta[..., None]
    m = jnp.where(strict, jnp.einsum('bhnid,bhnjd->bhnij', kb, k) * decay, 0.0)
    rhs = jnp.concatenate([v * beta[..., None], kb * jnp.exp(gc)[..., None]], axis=-1)
    sol = lax.linalg.triangular_solve(m, rhs, left_side=True, lower=True, unit_diagonal=True)
    u, w = sol[..., :DV], sol[..., DV:]
    attn = jnp.einsum('bhnid,bhnjd->bhnij', q, k) * decay
    qg = q * jnp.exp(gc)[..., None]
    kg = k * jnp.exp(gc[..., -1:] - gc)[..., None]
    g_last = jnp.exp(gc[..., -1])
    xs = tuple(jnp.moveaxis(t, 2, 0) for t in (qg, kg, u, w, attn, g_last))

    def step(state, inp):
        qg_c, kg_c, u_c, w_c, attn_c, gl_c = inp
        v_new = u_c - jnp.einsum('bhcd,bhdv->bhcv', w_c, state)
        o = jnp.einsum('bhcd,bhdv->bhcv', qg_c, state) + jnp.einsum('bhij,bhjv->bhiv', attn_c, v_new)
        state = state * gl_c[..., None, None] + jnp.einsum('bhcd,bhcv->bhdv', kg_c, v_new)
        return state, o

    _, o = lax.scan(step, jnp.zeros((B, H, DK, DV), jnp.float32), xs)
    o = jnp.moveaxis(o, 0, 2).reshape(B, H, S, DV)
    return jnp.moveaxis(o, 1, 2)


def gdn_mixer(qkv, z, a, b, conv_w, a_log, dt_bias, gdn_norm):
    B, S, _ = qkv.shape
    qkv = jax.nn.silu(depthwise_conv_centred(qkv, conv_w))
    qd = GDN_HEADS * GDN_DK
    q = l2_normalize(qkv[..., :qd].reshape(B, S, GDN_HEADS, GDN_DK))
    k = l2_normalize(qkv[..., qd:2 * qd].reshape(B, S, GDN_HEADS, GDN_DK))
    v = qkv[..., 2 * qd:].reshape(B, S, GDN_HEADS, GDN_DV)
    a = a.astype(jnp.float32).reshape(B, S, 2, GDN_HEADS)
    b = b.astype(jnp.float32).reshape(B, S, 2, GDN_HEADS)
    g = -jnp.exp(a_log.astype(jnp.float32)) * jax.nn.softplus(a + dt_bias.astype(jnp.float32))
    beta = jax.nn.sigmoid(b)
    o_fwd = gated_delta_chunked(q, k, v, g[:, :, 0], beta[:, :, 0])
    flip = lambda t: jnp.flip(t, axis=1)
    o_bwd = flip(gated_delta_chunked(flip(q), flip(k), flip(v), flip(g[:, :, 1]), flip(beta[:, :, 1])))
    o = (o_fwd + o_bwd).astype(qkv.dtype)
    o = rms_norm(o, gdn_norm) * jax.nn.silu(z.reshape(B, S, GDN_HEADS, GDN_DV))
    return o.reshape(B, S, GDN_HEADS * GDN_DV)


def trunk(x, norm_attn, w_in, q_norm, w_q_up, kv_norm, w_kv_up, conv_w, a_log, dt_bias,
          gdn_norm, w_out, norm_ffn, w_gate, w_up, w_down, norm_final):
    cos, sin = rope_tables(x.shape[1])
    for l in range(DEPTH):
        h = rms_norm(x, norm_attn[l]) @ w_in[l]
        q_lat, kv_lat, k_rope, qkv, z, a, b = jnp.split(h, IN_SPLIT_POINTS, axis=-1)
        y_mla = mla_mixer(q_lat, kv_lat, k_rope, q_norm[l], w_q_up[l], kv_norm[l], w_kv_up[l], cos, sin)
        y_gdn = gdn_mixer(qkv, z, a, b, conv_w[l], a_log[l], dt_bias[l], gdn_norm[l])
        x = x + jnp.concatenate([y_mla, y_gdn], axis=-1) @ w_out[l]
        hn = rms_norm(x, norm_ffn[l])
        x = x + (jax.nn.silu(hn @ w_gate[l]) * (hn @ w_up[l])) @ w_down[l]
    return rms_norm(x, norm_final)


def setup_inputs(seed: int = 0) -> dict:
    key = jax.random.key(seed)
    ks = jax.random.split(key, 20)
    f32 = jnp.float32

    def w(k, shape, fan_in):
        return jax.random.normal(k, shape, f32) * (fan_in ** -0.5)

    def gain(k, shape):
        return 1.0 + 0.02 * jax.random.normal(k, shape, f32)

    dt = jnp.exp(jax.random.uniform(ks[12], (DEPTH, 2, GDN_HEADS), f32, math.log(1e-3), math.log(1e-1)))
    return {
        "x_prompt": jax.random.normal(ks[0], (BATCH, SEQ, D_MODEL), f32),
        "x_sample": jax.random.normal(ks[1], (DEC_BATCH, DEC_SEQ, D_MODEL), f32),
        "norm_attn": gain(ks[2], (DEPTH, D_MODEL)),
        "w_in": w(ks[3], (DEPTH, D_MODEL, IN_COLS), D_MODEL),
        "q_norm": gain(ks[4], (DEPTH, Q_LORA)),
        "w_q_up": w(ks[5], (DEPTH, Q_LORA, MLA_HEADS * (MLA_NOPE + MLA_ROPE)), Q_LORA),
        "kv_norm": gain(ks[6], (DEPTH, KV_LORA)),
        "w_kv_up": w(ks[7], (DEPTH, KV_LORA, MLA_HEADS * (MLA_NOPE + MLA_V)), KV_LORA),
        "conv_w": w(ks[8], (DEPTH, CONV_WIDTH, QKV_CH), CONV_WIDTH),
        "a_log": jnp.log(jax.random.uniform(ks[9], (DEPTH, 2, GDN_HEADS), f32, 1.0, 16.0)),
        "dt_bias": dt + jnp.log(-jnp.expm1(-dt)),
        "gdn_norm": gain(ks[10], (DEPTH, GDN_DV)),
        "w_out": w(ks[11], (DEPTH, MIX_WIDTH, D_MODEL), MIX_WIDTH),
        "norm_ffn": gain(ks[13], (DEPTH, D_MODEL)),
        "w_gate": w(ks[14], (DEPTH, D_MODEL, D_FF), D_MODEL),
        "w_up": w(ks[15], (DEPTH, D_MODEL, D_FF), D_MODEL),
        "w_down": w(ks[16], (DEPTH, D_FF, D_MODEL), D_FF),
        "norm_final": gain(ks[17], (D_MODEL,)),
    }


def reference(x_prompt, x_sample, norm_attn, w_in, q_norm, w_q_up, kv_norm, w_kv_up, conv_w,
              a_log, dt_bias, gdn_norm, w_out, norm_ffn, w_gate, w_up, w_down, norm_final):
    y_prompt = trunk(x_prompt, norm_attn, w_in, q_norm, w_q_up, kv_norm, w_kv_up, conv_w, a_log,
                     dt_bias, gdn_norm, w_out, norm_ffn, w_gate, w_up, w_down, norm_final)
    y_sample = trunk(x_sample, norm_attn, w_in, q_norm, w_q_up, kv_norm, w_kv_up, conv_w, a_log,
                     dt_bias, gdn_norm, w_out, norm_ffn, w_gate, w_up, w_down, norm_final)
    return (y_prompt, y_sample)
```

```python
import functools
import math

import jax
import jax.numpy as jnp
from jax import lax
from jax.experimental import pallas as pl
from jax.experimental.pallas import tpu as pltpu

F32 = jnp.float32
BF16 = jnp.bfloat16

D_MODEL = 1024
MLA_HEADS = 8
MLA_NOPE = 64
MLA_ROPE = 32
MLA_V = 64
Q_LORA = 256
KV_LORA = 128
ROPE_THETA = 10000.0
GDN_HEADS = 8
GDN_DK = 64
GDN_DV = 64
CONV_WIDTH = 3
D_FF = 2816
NORM_EPS = 1e-6

LANE = 128
HEAD_BLOCK = 128
GDN_CHUNK = 64
GDN_PAIR = 2
VMEM_LIMIT = 56 * 1024 * 1024
FF_CHUNK = 1536

LAT_Q = 0
LAT_KV = Q_LORA
LAT_KR = LAT_KV + KV_LORA
LAT_KRS = LAT_KR + LANE
LAT_AB = LAT_KRS + LANE
LAT_W = LAT_AB + LANE
QKV_W = GDN_HEADS * (2 * GDN_DK + GDN_DV)
Z_W = GDN_HEADS * GDN_DV
IN_W = LAT_W + QKV_W + Z_W


def _dot(a, b):
    return jnp.dot(a.astype(BF16), b.astype(BF16), preferred_element_type=F32)


def _dot_nt(a, b):
    return lax.dot_general(a.astype(BF16), b.astype(BF16), (((1,), (1,)), ((), ())),
                           preferred_element_type=F32)


def _rms(x, g):
    return x * lax.rsqrt(jnp.mean(x * x, axis=-1, keepdims=True) + NORM_EPS) * g


def _silu(x):
    return x * jax.nn.sigmoid(x)


def _cparams(n_axes):
    return pltpu.CompilerParams(dimension_semantics=("arbitrary",) * n_axes,
                                vmem_limit_bytes=VMEM_LIMIT)


def _in_proj_kernel(x_ref, g_ref, w_ref, lat_ref, qkv_ref, z_ref):
    xn = _rms(x_ref[...], g_ref[...]).astype(BF16)
    lat_ref[...] = jnp.dot(xn, w_ref[:, :LAT_W], preferred_element_type=F32)
    qkv_ref[...] = jnp.dot(xn, w_ref[:, LAT_W:LAT_W + QKV_W], preferred_element_type=F32)
    z_ref[...] = jnp.dot(xn, w_ref[:, LAT_W + QKV_W:], preferred_element_type=F32)


def _in_proj(x, g, w, tm):
    t = x.shape[0]
    return pl.pallas_call(
        _in_proj_kernel,
        grid=(t // tm,),
        in_specs=[pl.BlockSpec((tm, D_MODEL), lambda i: (i, 0)),
                  pl.BlockSpec((1, D_MODEL), lambda i: (0, 0)),
                  pl.BlockSpec((D_MODEL, IN_W), lambda i: (0, 0))],
        out_specs=[pl.BlockSpec((tm, LAT_W), lambda i: (i, 0)),
                   pl.BlockSpec((tm, QKV_W), lambda i: (i, 0)),
                   pl.BlockSpec((tm, Z_W), lambda i: (i, 0))],
        out_shape=[jax.ShapeDtypeStruct((t, LAT_W), F32),
                   jax.ShapeDtypeStruct((t, QKV_W), F32),
                   jax.ShapeDtypeStruct((t, Z_W), F32)],
        compiler_params=_cparams(1),
        name="in_proj",
    )(x, g, w)


def _mla_proj_kernel(lat_ref, tab_ref, qn_ref, kvn_ref, wq_ref, wkv_ref, q_ref, k_ref, v_ref):
    hw = MLA_HEADS * HEAD_BLOCK
    qn = _rms(lat_ref[:, LAT_Q:LAT_Q + Q_LORA], qn_ref[...]).astype(BF16)
    qa = jnp.dot(qn, wq_ref[:, :hw], preferred_element_type=F32)
    qb = jnp.dot(qn, wq_ref[:, hw:], preferred_element_type=F32)
    ccq = tab_ref[:, 0:LANE]
    ssq = tab_ref[:, LANE:2 * LANE]
    cck = tab_ref[:, 2 * LANE:3 * LANE]
    ssk = tab_ref[:, 3 * LANE:4 * LANE]
    kvn = _rms(lat_ref[:, LAT_KV:LAT_KV + KV_LORA], kvn_ref[...]).astype(BF16)
    ka = jnp.dot(kvn, wkv_ref[:, :hw], preferred_element_type=F32)
    vv = jnp.dot(kvn, wkv_ref[:, hw:], preferred_element_type=F32)
    kpe = lat_ref[:, LAT_KR:LAT_KR + LANE] * cck + lat_ref[:, LAT_KRS:LAT_KRS + LANE] * ssk
    for h in range(MLA_HEADS):
        sl = slice(h * HEAD_BLOCK, (h + 1) * HEAD_BLOCK)
        q_ref[0, h] = (qa[:, sl] * ccq + qb[:, sl] * ssq).astype(BF16)
        k_ref[0, h] = (ka[:, sl] + kpe).astype(BF16)
    for p in range(MLA_HEADS // 2):
        v_ref[0, p] = vv[:, p * LANE:(p + 1) * LANE].astype(BF16)


def _mla_proj(lat, tab, qn, kvn, wq, wkv, b, s, tm):
    nt = s // tm
    hw = MLA_HEADS * HEAD_BLOCK
    return pl.pallas_call(
        _mla_proj_kernel,
        grid=(b, nt),
        in_specs=[pl.BlockSpec((tm, LAT_AB), lambda bi, si: (bi * nt + si, 0)),
                  pl.BlockSpec((tm, 4 * LANE), lambda bi, si: (si, 0)),
                  pl.BlockSpec((1, Q_LORA), lambda bi, si: (0, 0)),
                  pl.BlockSpec((1, KV_LORA), lambda bi, si: (0, 0)),
                  pl.BlockSpec((Q_LORA, 2 * hw), lambda bi, si: (0, 0)),
                  pl.BlockSpec((KV_LORA, hw + MLA_HEADS * MLA_V), lambda bi, si: (0, 0))],
        out_specs=[pl.BlockSpec((1, MLA_HEADS, tm, HEAD_BLOCK), lambda bi, si: (bi, 0, si, 0)),
                   pl.BlockSpec((1, MLA_HEADS, tm, HEAD_BLOCK), lambda bi, si: (bi, 0, si, 0)),
                   pl.BlockSpec((1, MLA_HEADS // 2, tm, LANE), lambda bi, si: (bi, 0, si, 0))],
        out_shape=[jax.ShapeDtypeStruct((b, MLA_HEADS, s, HEAD_BLOCK), BF16),
                   jax.ShapeDtypeStruct((b, MLA_HEADS, s, HEAD_BLOCK), BF16),
                   jax.ShapeDtypeStruct((b, MLA_HEADS // 2, s, LANE), BF16)],
        compiler_params=_cparams(2),
        name="mla_proj",
    )(lat, tab, qn, kvn, wq, wkv)


def _attn_kernel(q_ref, k_ref, v_ref, o_ref, *, kb):
    s = k_ref.shape[2]
    outs = []
    for h in range(2):
        q = q_ref[0, h]
        tq = q.shape[0]
        m = jnp.full((tq, 1), -jnp.inf, F32)
        l = jnp.zeros((tq, 1), F32)
        acc = jnp.zeros((tq, LANE), F32)
        for j in range(s // kb):
            k = k_ref[0, h, j * kb:(j + 1) * kb, :]
            sc = lax.dot_general(q, k, (((1,), (1,)), ((), ())), preferred_element_type=F32)
            m_new = jnp.maximum(m, jnp.max(sc, axis=-1, keepdims=True))
            alpha = jnp.exp(m - m_new)
            p = jnp.exp(sc - m_new)
            l = alpha * l + jnp.sum(p, axis=-1, keepdims=True)
            acc = alpha * acc + jnp.dot(p.astype(BF16), v_ref[0, 0, j * kb:(j + 1) * kb, :],
                                        preferred_element_type=F32)
            m = m_new
        outs.append(acc / l)
    lane = lax.broadcasted_iota(jnp.int32, outs[0].shape, 1)
    o_ref[...] = jnp.where(lane < MLA_V, outs[0], outs[1])


def _attention(q, k, v, tq, kb):
    b, h, s, _ = q.shape
    nq = s // tq
    return pl.pallas_call(
        functools.partial(_attn_kernel, kb=kb),
        grid=(b, h // 2, nq),
        in_specs=[pl.BlockSpec((1, 2, tq, HEAD_BLOCK), lambda bi, pi, qi: (bi, pi, qi, 0)),
                  pl.BlockSpec((1, 2, s, HEAD_BLOCK), lambda bi, pi, qi: (bi, pi, 0, 0)),
                  pl.BlockSpec((1, 1, s, LANE), lambda bi, pi, qi: (bi, pi, 0, 0))],
        out_specs=pl.BlockSpec((tq, LANE), lambda bi, pi, qi: (bi * nq + qi, pi)),
        out_shape=jax.ShapeDtypeStruct((b * s, h * MLA_V), F32),
        compiler_params=_cparams(3),
        name="mla_attention",
    )(q, k, v)


def _half_sums(x, lane_lo):
    s0 = jnp.sum(jnp.where(lane_lo, x, 0.0), axis=-1, keepdims=True)
    s1 = jnp.sum(jnp.where(lane_lo, 0.0, x), axis=-1, keepdims=True)
    return jnp.where(lane_lo, s0, s1)


def _gdn_kernel(q_ref, k_ref, v_ref, z_ref, ab_ref, cq_ref, ck_ref, cv_ref, alog_ref, dtb_ref,
                gn_ref, y_ref, qs, ks, vs, of_s, ob_s):
    s = q_ref.shape[0]
    c = GDN_CHUNK
    n = s // c
    c2 = 2 * c
    pid = pl.program_id(1)

    row_s = lax.broadcasted_iota(jnp.int32, (s, LANE), 0)
    lane_s = lax.broadcasted_iota(jnp.int32, (s, LANE), 1)
    lo_s = lane_s < GDN_DK

    def conv_silu(x_ref, w_ref):
        x = x_ref[...]
        prev = jnp.where(row_s == 0, 0.0, pltpu.roll(x, 1, 0))
        nxt = jnp.where(row_s == s - 1, 0.0, pltpu.roll(x, s - 1, 0))
        return _silu(prev * w_ref[0:1, :] + x * w_ref[1:2, :] + nxt * w_ref[2:3, :])

    def l2n(x):
        return x * lax.rsqrt(_half_sums(x * x, lo_s) + NORM_EPS)

    qs[...] = l2n(conv_silu(q_ref, cq_ref)) * (GDN_DK ** -0.5)
    ks[...] = l2n(conv_silu(k_ref, ck_ref))
    vs[...] = conv_silu(v_ref, cv_ref)

    ri = lax.broadcasted_iota(jnp.int32, (c2, c2), 0)
    ci = lax.broadcasted_iota(jnp.int32, (c2, c2), 1)
    same_head = (ri // c) == (ci // c)
    lane_c = lax.broadcasted_iota(jnp.int32, (c, LANE), 1)
    lo_c = lane_c < GDN_DK
    top = ri < c
    eye = (ri == ci).astype(F32)

    def stack(x):
        return jnp.concatenate([jnp.where(lo_c, x, 0.0), jnp.where(lo_c, 0.0, x)], axis=0)

    def col(x, lane_idx):
        return jnp.sum(jnp.where(lane_c == lane_idx, x, 0.0), axis=-1, keepdims=True)

    def prep(ci_chunk, d):
        r0 = pl.multiple_of(ci_chunk * c, c)
        kst = stack(ks[pl.ds(r0, c), :])
        qst = stack(qs[pl.ds(r0, c), :])
        vst = stack(vs[pl.ds(r0, c), :])
        ab = ab_ref[pl.ds(r0, c), :]
        t = ab + dtb_ref[...]
        softplus = jnp.maximum(t, 0.0) + jnp.log(1.0 + jnp.exp(-jnp.abs(t)))
        g16 = -jnp.exp(alog_ref[...]) * softplus
        b16 = jax.nn.sigmoid(ab)
        h0 = d * GDN_HEADS + GDN_PAIR * pid
        g_b = jnp.concatenate([jnp.broadcast_to(col(g16, h0), (c, LANE)),
                               jnp.broadcast_to(col(g16, h0 + 1), (c, LANE))], axis=0)
        beta_b = jnp.concatenate(
            [jnp.broadcast_to(col(b16, 2 * GDN_HEADS + h0), (c, LANE)),
             jnp.broadcast_to(col(b16, 2 * GDN_HEADS + h0 + 1), (c, LANE))], axis=0)
        if d == 0:
            incl = same_head & (ri >= ci)
            strict = same_head & (ri > ci)
            last = c - 1
        else:
            incl = same_head & (ri <= ci)
            strict = same_head & (ri < ci)
            last = 0
        g_hi = g_b.astype(BF16)
        g_lo = (g_b - g_hi.astype(F32)).astype(BF16)
        zero = jnp.zeros((), BF16)
        sel_t = same_head & ((ri > ci) if d == 0 else (ri < ci))
        rhs = jnp.concatenate([jnp.where(sel_t, g_hi, zero), g_hi,
                               jnp.where(sel_t, g_lo, zero), g_lo], axis=1)
        cum = jnp.dot(incl.astype(BF16), rhs, preferred_element_type=F32)
        dlog = cum[:, 0:c2] + cum[:, 2 * c2:3 * c2]
        gc = cum[:, c2:2 * c2] + cum[:, 3 * c2:4 * c2]
        decay = jnp.where(incl, jnp.exp(jnp.where(incl, dlog, 0.0)), 0.0)
        eg = jnp.exp(gc)
        gl = jnp.where(top, gc[last:last + 1, :], gc[c + last:c + last + 1, :])
        kk = _dot_nt(kst, kst)
        qk = _dot_nt(qst, kst)
        mm = jnp.where(strict, kk * decay * beta_b, 0.0)
        attn = qk * decay
        tinv = eye - jnp.where((ri // 2) == (ci // 2), mm, 0.0)
        sz = 2
        while sz < c:
            nmask = ((ri // (2 * sz)) == (ci // (2 * sz))) & ((ri // sz) != (ci // sz))
            ns = jnp.where(nmask, mm, 0.0)
            tinv = tinv - _dot(tinv, _dot(ns, tinv))
            sz *= 2
        u = _dot(tinv, vst * beta_b)
        w = _dot(tinv, kst * beta_b * eg)
        qg = qst * eg
        kg = kst * jnp.exp(gl - gc)
        return u, w.astype(BF16), qg.astype(BF16), kg.astype(BF16), attn.astype(BF16), jnp.exp(gl)

    def step(state, pre):
        u, w, qg, kg, attn, egl = pre
        sb = state.astype(BF16)
        v_new = u - jnp.dot(w, sb, preferred_element_type=F32)
        o = jnp.dot(qg, sb, preferred_element_type=F32) + _dot(attn, v_new)
        state = state * egl + jnp.dot(kg.T, v_new.astype(BF16), preferred_element_type=F32)
        return state, o[:c, :] + o[c:, :]

    def body(i, carry):
        sf, sb_ = carry
        cf = i
        cb = n - 1 - i
        sf, o_f = step(sf, prep(cf, 0))
        sb_, o_b = step(sb_, prep(cb, 1))
        of_s[pl.ds(pl.multiple_of(cf * c, c), c), :] = o_f
        ob_s[pl.ds(pl.multiple_of(cb * c, c), c), :] = o_b
        return sf, sb_

    zero_state = jnp.zeros((c2, c2), F32)
    lax.fori_loop(0, n, body, (zero_state, zero_state))

    o = of_s[...] + ob_s[...]
    ms = _half_sums(o * o, lo_s) * (1.0 / GDN_DV)
    y_ref[...] = o * lax.rsqrt(ms + NORM_EPS) * gn_ref[...] * _silu(z_ref[...])


def _gdn(qkv, z, lat, conv_w, alog_l, dtb_l, gn_l, b, s):
    npair = GDN_HEADS // GDN_PAIR
    nqk = GDN_HEADS * GDN_DK // LANE
    seq_blk = lambda off: pl.BlockSpec((s, LANE), lambda bi, pi: (bi, off + pi))
    conv_blk = lambda off: pl.BlockSpec((CONV_WIDTH, LANE), lambda bi, pi: (0, off + pi))
    vec_blk = pl.BlockSpec((1, LANE), lambda bi, pi: (0, 0))
    return pl.pallas_call(
        _gdn_kernel,
        grid=(b, npair),
        in_specs=[seq_blk(0), seq_blk(nqk), seq_blk(2 * nqk),
                  pl.BlockSpec((s, LANE), lambda bi, pi: (bi, pi)),
                  pl.BlockSpec((s, LANE), lambda bi, pi: (bi, LAT_AB // LANE)),
                  conv_blk(0), conv_blk(nqk), conv_blk(2 * nqk),
                  vec_blk, vec_blk, vec_blk],
        out_specs=pl.BlockSpec((s, LANE), lambda bi, pi: (bi, pi)),
        out_shape=jax.ShapeDtypeStruct((b * s, GDN_HEADS * GDN_DV), F32),
        scratch_shapes=[pltpu.VMEM((s, LANE), F32)] * 5,
        compiler_params=_cparams(2),
        name="gdn_mixer",
    )(qkv, qkv, qkv, z, lat, conv_w, conv_w, conv_w, alog_l, dtb_l, gn_l)


def _out_ffn_kernel(x_ref, ym_ref, yg_ref, wo_ref, nf_ref, wg_ref, wu_ref, wd_ref, nfin_ref,
                    o_ref, *, final, ff_chunk):
    y = jnp.concatenate([ym_ref[...].astype(BF16), yg_ref[...].astype(BF16)], axis=1)
    x = x_ref[...] + jnp.dot(y, wo_ref[...], preferred_element_type=F32)
    hn = _rms(x, nf_ref[...]).astype(BF16)
    o_ref[...] = x
    for lo in range(0, D_FF, ff_chunk):
        sl = slice(lo, min(lo + ff_chunk, D_FF))
        g = jnp.dot(hn, wg_ref[:, sl], preferred_element_type=F32)
        u = jnp.dot(hn, wu_ref[:, sl], preferred_element_type=F32)
        o_ref[...] += jnp.dot((_silu(g) * u).astype(BF16), wd_ref[sl, :], preferred_element_type=F32)
    if final:
        o_ref[...] = _rms(o_ref[...], nfin_ref[...])


def _out_ffn(x, ym, yg, wo, nf, wg, wu, wd, nfin, tm, final):
    t = x.shape[0]
    const = lambda shape: pl.BlockSpec(shape, lambda i: (0, 0), pipeline_mode=pl.Buffered(1))
    return pl.pallas_call(
        functools.partial(_out_ffn_kernel, final=final, ff_chunk=FF_CHUNK),
        grid=(t // tm,),
        in_specs=[pl.BlockSpec((tm, D_MODEL), lambda i: (i, 0)),
                  pl.BlockSpec((tm, ym.shape[1]), lambda i: (i, 0)),
                  pl.BlockSpec((tm, yg.shape[1]), lambda i: (i, 0)),
                  const(wo.shape), const((1, D_MODEL)), const(wg.shape), const(wu.shape),
                  const(wd.shape), const((1, D_MODEL))],
        out_specs=pl.BlockSpec((tm, D_MODEL), lambda i: (i, 0)),
        out_shape=jax.ShapeDtypeStruct((t, D_MODEL), F32),
        compiler_params=_cparams(1),
        name="out_ffn",
    )(x, ym, yg, wo, nf, wg, wu, wd, nfin)


def _swap_halves(w):
    r = w.shape[-1] // 2
    return jnp.concatenate([w[..., r:], w[..., :r]], axis=-1)


def _prep_layer(w_in, w_q_up, w_kv_up):
    d = w_in.shape[0]
    o_kv = Q_LORA
    o_kr = o_kv + KV_LORA
    o_qkv = o_kr + MLA_ROPE
    o_z = o_qkv + QKV_W
    o_a = o_z + Z_W
    kr = w_in[:, o_kr:o_qkv]
    pad_rope = lambda w: jnp.concatenate(
        [jnp.zeros((d, MLA_NOPE), F32), w, jnp.zeros((d, HEAD_BLOCK - MLA_NOPE - MLA_ROPE), F32)], axis=1)
    ab = jnp.concatenate([w_in[:, o_a:], jnp.zeros((d, LANE - 4 * GDN_HEADS), F32)], axis=1)
    w_in_p = jnp.concatenate([w_in[:, :o_kr], pad_rope(kr), pad_rope(_swap_halves(kr)), ab,
                              w_in[:, o_qkv:o_z], w_in[:, o_z:o_a]], axis=1).astype(BF16)

    dq = MLA_NOPE + MLA_ROPE
    wq = w_q_up.reshape(Q_LORA, MLA_HEADS, dq)
    zpad = jnp.zeros((Q_LORA, MLA_HEADS, HEAD_BLOCK - dq), F32)
    wq_a = jnp.concatenate([wq, zpad], axis=-1)
    wq_b = jnp.concatenate([jnp.zeros((Q_LORA, MLA_HEADS, MLA_NOPE), F32),
                            _swap_halves(wq[..., MLA_NOPE:]), zpad], axis=-1)
    wq_p = jnp.concatenate([wq_a.reshape(Q_LORA, -1), wq_b.reshape(Q_LORA, -1)], axis=1).astype(BF16)

    wkv = w_kv_up.reshape(KV_LORA, MLA_HEADS, MLA_NOPE + MLA_V)
    wk = jnp.concatenate([wkv[..., :MLA_NOPE],
                          jnp.zeros((KV_LORA, MLA_HEADS, HEAD_BLOCK - MLA_NOPE), F32)], axis=-1)
    wkv_p = jnp.concatenate([wk.reshape(KV_LORA, -1),
                             wkv[..., MLA_NOPE:].reshape(KV_LORA, -1)], axis=1).astype(BF16)
    return w_in_p, wq_p, wkv_p


def _rope_table(s):
    pos = jnp.arange(s, dtype=F32)
    inv = ROPE_THETA ** (-jnp.arange(0, MLA_ROPE, 2, dtype=F32) / MLA_ROPE)
    ang = pos[:, None] * inv[None, :]
    c, sn = jnp.cos(ang), jnp.sin(ang)
    cc = jnp.concatenate([c, c], axis=1)
    ss = jnp.concatenate([-sn, sn], axis=1)
    z_lo = jnp.zeros((s, MLA_NOPE), F32)
    z_hi = jnp.zeros((s, HEAD_BLOCK - MLA_NOPE - MLA_ROPE), F32)
    scale = (MLA_NOPE + MLA_ROPE) ** -0.5
    ccq = jnp.concatenate([jnp.ones((s, MLA_NOPE), F32), cc, z_hi], axis=1) * scale
    ssq = jnp.concatenate([z_lo, ss, z_hi], axis=1) * scale
    cck = jnp.concatenate([z_lo, cc, z_hi], axis=1)
    ssk = jnp.concatenate([z_lo, ss, z_hi], axis=1)
    return jnp.concatenate([ccq, ssq, cck, ssk], axis=1)


def _lane_vec(v):
    v = v.reshape(1, -1).astype(F32)
    return jnp.concatenate([v, jnp.zeros((1, LANE - v.shape[1]), F32)], axis=1)


def _pick_tile(n, pref):
    t = min(pref, n)
    while n % t:
        t //= 2
    return t


def _trunk(x, layers, norm_final, tab):
    b, s, d = x.shape
    t = b * s
    tm = _pick_tile(s, 512)
    tq = _pick_tile(s, 256)
    kb = _pick_tile(s, 1024)
    h = x.reshape(t, d)
    depth = len(layers)
    for li, lw in enumerate(layers):
        lat, qkv, z = _in_proj(h, lw["norm_attn"], lw["w_in"], tm)
        q, k, v = _mla_proj(lat, tab, lw["q_norm"], lw["kv_norm"], lw["wq"], lw["wkv"], b, s, tm)
        y_mla = _attention(q, k, v, tq, kb)
        y_gdn = _gdn(qkv, z, lat, lw["conv_w"], lw["a_log"], lw["dt_bias"], lw["gdn_norm"], b, s)
        h = _out_ffn(h, y_mla, y_gdn, lw["w_out"], lw["norm_ffn"], lw["w_gate"], lw["w_up"],
                     lw["w_down"], norm_final, tm, final=(li == depth - 1))
    return h.reshape(b, s, d)


def kernel(x_prompt, x_sample, norm_attn, w_in, q_norm, w_q_up, kv_norm, w_kv_up, conv_w, a_log, dt_bias, gdn_norm, w_out, norm_ffn, w_gate, w_up, w_down, norm_final):
    depth = w_in.shape[0]
    layers = []
    for l in range(depth):
        w_in_p, wq_p, wkv_p = _prep_layer(w_in[l], w_q_up[l], w_kv_up[l])
        layers.append(dict(
            norm_attn=norm_attn[l].reshape(1, -1), w_in=w_in_p,
            q_norm=q_norm[l].reshape(1, -1), kv_norm=kv_norm[l].reshape(1, -1),
            wq=wq_p, wkv=wkv_p, conv_w=conv_w[l],
            a_log=_lane_vec(a_log[l]), dt_bias=_lane_vec(dt_bias[l]),
            gdn_norm=jnp.tile(gdn_norm[l].reshape(1, -1), (1, GDN_PAIR)),
            w_out=w_out[l].astype(BF16), norm_ffn=norm_ffn[l].reshape(1, -1),
            w_gate=w_gate[l].astype(BF16), w_up=w_up[l].astype(BF16),
            w_down=w_down[l].astype(BF16)))
    nfin = norm_final.reshape(1, -1)
    assert x_prompt.shape[1] == x_sample.shape[1]
    tab = _rope_table(x_prompt.shape[1])
    y_prompt = _trunk(x_prompt, layers, nfin, tab)
    y_sample = _trunk(x_sample, layers, nfin, tab)
    return (y_prompt, y_sample)
```

```python
import functools
import math

import jax
import jax.numpy as jnp
from jax import lax
from jax.experimental import pallas as pl
from jax.experimental.pallas import tpu as pltpu

F32 = jnp.float32
BF16 = jnp.bfloat16

D_MODEL = 1024
MLA_HEADS = 8
MLA_NOPE = 64
MLA_ROPE = 32
MLA_V = 64
Q_LORA = 256
KV_LORA = 128
ROPE_THETA = 10000.0
GDN_HEADS = 8
GDN_DK = 64
GDN_DV = 64
CONV_WIDTH = 3
D_FF = 2816
NORM_EPS = 1e-6

LANE = 128
HEAD_BLOCK = 128
GDN_CHUNK = 64
GDN_PAIR = 2
VMEM_LIMIT = 56 * 1024 * 1024
FF_CHUNK = 1536
GDN_UNROLL = 8

LAT_Q = 0
LAT_KV = Q_LORA
LAT_KR = LAT_KV + KV_LORA
LAT_KRS = LAT_KR + LANE
LAT_AB = LAT_KRS + LANE
LAT_W = LAT_AB + LANE
QKV_W = GDN_HEADS * (2 * GDN_DK + GDN_DV)
Z_W = GDN_HEADS * GDN_DV
IN_W = LAT_W + QKV_W + Z_W


def _dot(a, b):
    return jnp.dot(a.astype(BF16), b.astype(BF16), preferred_element_type=F32)


def _dot_nt(a, b):
    return lax.dot_general(a.astype(BF16), b.astype(BF16), (((1,), (1,)), ((), ())),
                           preferred_element_type=F32)


def _rms(x, g):
    return x * lax.rsqrt(jnp.mean(x * x, axis=-1, keepdims=True) + NORM_EPS) * g


def _silu(x):
    return x * jax.nn.sigmoid(x)


def _cparams(n_axes):
    return pltpu.CompilerParams(dimension_semantics=("arbitrary",) * n_axes,
                                vmem_limit_bytes=VMEM_LIMIT)


def _in_proj_kernel(x_ref, g_ref, w_ref, lat_ref, qkv_ref, z_ref):
    xn = _rms(x_ref[...], g_ref[...]).astype(BF16)
    lat_ref[...] = jnp.dot(xn, w_ref[:, :LAT_W], preferred_element_type=F32)
    qkv_ref[...] = jnp.dot(xn, w_ref[:, LAT_W:LAT_W + QKV_W], preferred_element_type=F32)
    z_ref[...] = jnp.dot(xn, w_ref[:, LAT_W + QKV_W:], preferred_element_type=F32)


def _in_proj(x, g, w, tm):
    t = x.shape[0]
    return pl.pallas_call(
        _in_proj_kernel,
        grid=(t // tm,),
        in_specs=[pl.BlockSpec((tm, D_MODEL), lambda i: (i, 0)),
                  pl.BlockSpec((1, D_MODEL), lambda i: (0, 0)),
                  pl.BlockSpec((D_MODEL, IN_W), lambda i: (0, 0))],
        out_specs=[pl.BlockSpec((tm, LAT_W), lambda i: (i, 0)),
                   pl.BlockSpec((tm, QKV_W), lambda i: (i, 0)),
                   pl.BlockSpec((tm, Z_W), lambda i: (i, 0))],
        out_shape=[jax.ShapeDtypeStruct((t, LAT_W), F32),
                   jax.ShapeDtypeStruct((t, QKV_W), F32),
                   jax.ShapeDtypeStruct((t, Z_W), F32)],
        compiler_params=_cparams(1),
        name="in_proj",
    )(x, g, w)


def _mla_proj_kernel(lat_ref, tab_ref, qn_ref, kvn_ref, wq_ref, wk_ref, wvt_ref, q_ref, k_ref, vt_ref):
    hw = MLA_HEADS * HEAD_BLOCK
    qn = _rms(lat_ref[:, LAT_Q:LAT_Q + Q_LORA], qn_ref[...]).astype(BF16)
    qa = jnp.dot(qn, wq_ref[:, :hw], preferred_element_type=F32)
    qb = jnp.dot(qn, wq_ref[:, hw:], preferred_element_type=F32)
    ccq = tab_ref[:, 0:LANE]
    ssq = tab_ref[:, LANE:2 * LANE]
    cck = tab_ref[:, 2 * LANE:3 * LANE]
    ssk = tab_ref[:, 3 * LANE:4 * LANE]
    kvn = _rms(lat_ref[:, LAT_KV:LAT_KV + KV_LORA], kvn_ref[...]).astype(BF16)
    ka = jnp.dot(kvn, wk_ref[...], preferred_element_type=F32)
    vt = lax.dot_general(wvt_ref[...], kvn, (((1,), (1,)), ((), ())), preferred_element_type=F32)
    kpe = lat_ref[:, LAT_KR:LAT_KR + LANE] * cck + lat_ref[:, LAT_KRS:LAT_KRS + LANE] * ssk
    for h in range(MLA_HEADS):
        sl = slice(h * HEAD_BLOCK, (h + 1) * HEAD_BLOCK)
        q_ref[0, h] = (qa[:, sl] * ccq + qb[:, sl] * ssq).astype(BF16)
        k_ref[0, h] = (ka[:, sl] + kpe).astype(BF16)
    for p in range(MLA_HEADS // 2):
        vt_ref[0, p] = vt[p * LANE:(p + 1) * LANE, :].astype(BF16)


def _mla_proj(lat, tab, qn, kvn, wq, wk, wvt, b, s, tm):
    nt = s // tm
    hw = MLA_HEADS * HEAD_BLOCK
    return pl.pallas_call(
        _mla_proj_kernel,
        grid=(b, nt),
        in_specs=[pl.BlockSpec((tm, LAT_AB), lambda bi, si: (bi * nt + si, 0)),
                  pl.BlockSpec((tm, 4 * LANE), lambda bi, si: (si, 0)),
                  pl.BlockSpec((1, Q_LORA), lambda bi, si: (0, 0)),
                  pl.BlockSpec((1, KV_LORA), lambda bi, si: (0, 0)),
                  pl.BlockSpec((Q_LORA, 2 * hw), lambda bi, si: (0, 0)),
                  pl.BlockSpec((KV_LORA, hw), lambda bi, si: (0, 0)),
                  pl.BlockSpec((MLA_HEADS * MLA_V, KV_LORA), lambda bi, si: (0, 0))],
        out_specs=[pl.BlockSpec((1, MLA_HEADS, tm, HEAD_BLOCK), lambda bi, si: (bi, 0, si, 0)),
                   pl.BlockSpec((1, MLA_HEADS, tm, HEAD_BLOCK), lambda bi, si: (bi, 0, si, 0)),
                   pl.BlockSpec((1, MLA_HEADS // 2, LANE, tm), lambda bi, si: (bi, 0, 0, si))],
        out_shape=[jax.ShapeDtypeStruct((b, MLA_HEADS, s, HEAD_BLOCK), BF16),
                   jax.ShapeDtypeStruct((b, MLA_HEADS, s, HEAD_BLOCK), BF16),
                   jax.ShapeDtypeStruct((b, MLA_HEADS // 2, LANE, s), BF16)],
        compiler_params=_cparams(2),
        name="mla_proj",
    )(lat, tab, qn, kvn, wq, wk, wvt)


def _attn_kernel(q_ref, k_ref, vt_ref, o_ref, *, kb):
    s = k_ref.shape[2]
    tq = q_ref.shape[2]
    units = [(h, j) for h in range(2) for j in range(s // kb)]

    def scores(unit):
        h, j = unit
        return lax.dot_general(k_ref[0, h, j * kb:(j + 1) * kb, :], q_ref[0, h],
                               (((1,), (1,)), ((), ())), preferred_element_type=F32)

    stats = [(jnp.full((1, tq), -jnp.inf, F32), jnp.zeros((1, tq), F32), jnp.zeros((MLA_V, tq), F32))] * 2
    sc_next = scores(units[0])
    for idx, (h, j) in enumerate(units):
        sc = sc_next
        if idx + 1 < len(units):
            sc_next = scores(units[idx + 1])
        m, l, acc = stats[h]
        m_new = jnp.maximum(m, jnp.max(sc, axis=0, keepdims=True))
        alpha = jnp.exp2(m - m_new)
        p = jnp.exp2(sc - m_new)
        l = alpha * l + jnp.sum(p, axis=0, keepdims=True)
        vt = vt_ref[0, 0, h * MLA_V:(h + 1) * MLA_V, j * kb:(j + 1) * kb]
        acc = alpha * acc + jnp.dot(vt, p.astype(BF16), preferred_element_type=F32)
        stats[h] = (m_new, l, acc)
    o_ref[...] = jnp.concatenate([acc / l for _, l, acc in stats], axis=0).T


def _attention(q, k, v, tq, kb):
    b, h, s, _ = q.shape
    nq = s // tq
    return pl.pallas_call(
        functools.partial(_attn_kernel, kb=kb),
        grid=(b, h // 2, nq),
        in_specs=[pl.BlockSpec((1, 2, tq, HEAD_BLOCK), lambda bi, pi, qi: (bi, pi, qi, 0)),
                  pl.BlockSpec((1, 2, s, HEAD_BLOCK), lambda bi, pi, qi: (bi, pi, 0, 0)),
                  pl.BlockSpec((1, 1, LANE, s), lambda bi, pi, qi: (bi, pi, 0, 0))],
        out_specs=pl.BlockSpec((tq, LANE), lambda bi, pi, qi: (bi * nq + qi, pi)),
        out_shape=jax.ShapeDtypeStruct((b * s, h * MLA_V), F32),
        compiler_params=_cparams(3),
        name="mla_attention",
    )(q, k, v)


def _half_sums(x, lane_lo):
    s0 = jnp.sum(jnp.where(lane_lo, x, 0.0), axis=-1, keepdims=True)
    s1 = jnp.sum(jnp.where(lane_lo, 0.0, x), axis=-1, keepdims=True)
    return jnp.where(lane_lo, s0, s1)


def _pair(a, b):
    return jnp.concatenate([a, b], axis=1)


def _block_diag(x2):
    w = x2.shape[1] // 2
    z = jnp.zeros((x2.shape[0], w), x2.dtype)
    return jnp.concatenate([jnp.concatenate([x2[:, :w], z], axis=1),
                            jnp.concatenate([z, x2[:, w:]], axis=1)], axis=0)


def _dot_pair(a2, b2):
    return jnp.dot(a2.astype(BF16), _block_diag(b2.astype(BF16)), preferred_element_type=F32)


def _gdn_kernel(q_ref, k_ref, v_ref, z_ref, ab_ref, cq_ref, ck_ref, cv_ref, alog_ref, dtb_ref,
                gn_ref, y_ref, qs, ks, vs, of_s, ob_s, *, unroll):
    s = q_ref.shape[0]
    c = GDN_CHUNK
    n = s // c
    c2 = 2 * c
    pid = pl.program_id(1)

    row_s = lax.broadcasted_iota(jnp.int32, (s, LANE), 0)
    lane_s = lax.broadcasted_iota(jnp.int32, (s, LANE), 1)
    lo_s = lane_s < GDN_DK

    def conv_silu(x_ref, w_ref):
        x = x_ref[...]
        prev = jnp.where(row_s == 0, 0.0, pltpu.roll(x, 1, 0))
        nxt = jnp.where(row_s == s - 1, 0.0, pltpu.roll(x, s - 1, 0))
        return _silu(prev * w_ref[0:1, :] + x * w_ref[1:2, :] + nxt * w_ref[2:3, :])

    def l2n(x):
        return x * lax.rsqrt(_half_sums(x * x, lo_s) + NORM_EPS)

    qs[...] = l2n(conv_silu(q_ref, cq_ref)) * (GDN_DK ** -0.5)
    ks[...] = l2n(conv_silu(k_ref, ck_ref))
    vs[...] = conv_silu(v_ref, cv_ref)

    ri = lax.broadcasted_iota(jnp.int32, (c2, 2 * c2), 0)
    cj = lax.broadcasted_iota(jnp.int32, (c2, 2 * c2), 1)
    is_b = cj >= c2
    cc = jnp.where(is_b, cj - c2, cj)
    same_head = (ri // c) == (cc // c)
    incl = same_head & ((is_b & (ri <= cc)) | (jnp.logical_not(is_b) & (ri >= cc)))
    strict = incl & (ri != cc)
    eye = (ri == cc).astype(F32)
    tri = incl.astype(BF16)
    top = ri < c
    top1 = top[:, :c2]
    lane_c = lax.broadcasted_iota(jnp.int32, (c, LANE), 1)
    lo_c = lane_c < GDN_DK
    zero_bf = jnp.zeros((c2, c2), BF16)

    def stack(x):
        return jnp.concatenate([jnp.where(lo_c, x, 0.0), jnp.where(lo_c, 0.0, x)], axis=0)

    def col(x, lane_idx):
        return jnp.sum(jnp.where(lane_c == lane_idx, x, 0.0), axis=-1, keepdims=True)

    def gates(chunk, d):
        ab = ab_ref[pl.ds(pl.multiple_of(chunk * c, c), c), :]
        t = ab + dtb_ref[...]
        softplus = jnp.maximum(t, 0.0) + jnp.log(1.0 + jnp.exp(-jnp.abs(t)))
        g16 = -jnp.exp(alog_ref[...]) * softplus
        b16 = jax.nn.sigmoid(ab)
        h0 = d * GDN_HEADS + GDN_PAIR * pid
        bc = lambda x, i: jnp.broadcast_to(col(x, i), (c, LANE))
        g = jnp.concatenate([bc(g16, h0), bc(g16, h0 + 1)], axis=0)
        beta = jnp.concatenate([bc(b16, 2 * GDN_HEADS + h0), bc(b16, 2 * GDN_HEADS + h0 + 1)], axis=0)
        return g, beta

    def prep(cf, cb):
        rf = pl.multiple_of(cf * c, c)
        rb = pl.multiple_of(cb * c, c)
        k_f, k_b = stack(ks[pl.ds(rf, c), :]), stack(ks[pl.ds(rb, c), :])
        q_f, q_b = stack(qs[pl.ds(rf, c), :]), stack(qs[pl.ds(rb, c), :])
        k2 = _pair(k_f, k_b)
        q2 = _pair(q_f, q_b)
        v2 = _pair(stack(vs[pl.ds(rf, c), :]), stack(vs[pl.ds(rb, c), :]))
        g_f, beta_f = gates(cf, 0)
        g_b, beta_b = gates(cb, 1)
        beta2 = _pair(beta_f, beta_b)
        ghl = []
        for g in (g_f, g_b):
            hi = g.astype(BF16)
            ghl.append((hi, (g - hi.astype(F32)).astype(BF16)))
        wcum = jnp.concatenate(
            [jnp.concatenate([ghl[0][0], ghl[0][1], zero_bf, zero_bf], axis=1),
             jnp.concatenate([zero_bf, zero_bf, ghl[1][0], ghl[1][1]], axis=1)], axis=0)
        cum = jnp.dot(tri, wcum, preferred_element_type=F32)
        gram = _dot_nt(jnp.concatenate([k2, q2], axis=0), _block_diag(k2))
        yield
        gc_f = cum[:, 0:c2] + cum[:, c2:2 * c2]
        gc_b = cum[:, 2 * c2:3 * c2] + cum[:, 3 * c2:4 * c2]
        gc2 = _pair(gc_f, gc_b)
        dlog = _pair(gc_f - gc_f.T, gc_b - gc_b.T)
        decay = jnp.where(incl, jnp.exp(jnp.where(incl, dlog, 0.0)), 0.0)
        eg = jnp.exp(gc2)
        gl2 = _pair(jnp.where(top1, gc_f[c - 1:c, :], gc_f[c2 - 1:c2, :]),
                    jnp.where(top1, gc_b[0:1, :], gc_b[c:c + 1, :]))
        kk = gram[:c2]
        qk = gram[c2:]
        mm = jnp.where(strict, kk * decay * beta2, 0.0)
        attn = qk * decay
        tinv = eye - jnp.where((ri // 2) == (cc // 2), mm, 0.0)
        sz = 2
        while sz < c:
            nmask = ((ri // (2 * sz)) == (cc // (2 * sz))) & ((ri // sz) != (cc // sz))
            nt = _dot_pair(jnp.where(nmask, mm, 0.0), tinv)
            yield
            tinv = tinv - _dot_pair(tinv, nt)
            yield
            sz *= 2
        vb = (v2 * beta2).astype(BF16)
        kbe = (k2 * beta2 * eg).astype(BF16)
        rhs = jnp.concatenate(
            [jnp.concatenate([vb[:, :c2], kbe[:, :c2], zero_bf, zero_bf], axis=1),
             jnp.concatenate([zero_bf, zero_bf, vb[:, c2:], kbe[:, c2:]], axis=1)], axis=0)
        uw = jnp.dot(tinv.astype(BF16), rhs, preferred_element_type=F32).astype(BF16)
        yield
        kg = k2 * jnp.exp(gl2 - gc2)
        lhs = jnp.concatenate([_pair(kg[:, :c2].T, kg[:, c2:].T), attn], axis=0).astype(BF16)
        zz = jnp.zeros((c2, 2 * c2), BF16)
        res = jnp.dot(lhs, jnp.concatenate([jnp.concatenate([uw[:, :2 * c2], zz], axis=1),
                                            jnp.concatenate([zz, uw[:, 2 * c2:]], axis=1)], axis=0),
                      preferred_element_type=F32)
        yield
        b2 = _pair(res[:c2, 0:c2], res[:c2, 2 * c2:3 * c2])
        wk = _pair(res[:c2, c2:2 * c2], res[:c2, 3 * c2:4 * c2])
        o2 = _pair(res[c2:, 0:c2], res[c2:, 2 * c2:3 * c2])
        aw = _pair(res[c2:, c2:2 * c2], res[c2:, 3 * c2:4 * c2])
        qp = q2 * eg - aw
        return jnp.concatenate([wk, qp], axis=0).astype(BF16), b2, o2, jnp.exp(gl2)

    def scan(base, state, pres):
        for j, (lhs, b2, o2, egl) in enumerate(pres):
            cf = base + j
            cb = n - 1 - cf
            r = jnp.dot(lhs, _block_diag(state.astype(BF16)), preferred_element_type=F32)
            out = r[c2:] + o2
            state = state * egl + b2 - r[:c2]
            of_s[pl.ds(pl.multiple_of(cf * c, c), c), :] = out[:c, :c2] + out[c:, :c2]
            ob_s[pl.ds(pl.multiple_of(cb * c, c), c), :] = out[:c, c2:] + out[c:, c2:]
            yield
        return state

    def run_lockstep(gens):
        results = [None] * len(gens)
        live = list(range(len(gens)))
        while live:
            for i in list(live):
                try:
                    next(gens[i])
                except StopIteration as stop:
                    results[i] = stop.value
                    live.remove(i)
        return results

    n_groups = n // unroll

    def preps(group):
        base = group * unroll
        return [prep(base + j, n - 1 - (base + j)) for j in range(unroll)]

    def body(g, carry):
        state, pres = carry
        out = run_lockstep(preps(g) + [scan((g - 1) * unroll, state, pres)])
        return out[-1], out[:-1]

    state, pres = lax.fori_loop(1, n_groups, body,
                                (jnp.zeros((c2, 2 * c2), F32), run_lockstep(preps(0))))
    run_lockstep([scan((n_groups - 1) * unroll, state, pres)])

    o = of_s[...] + ob_s[...]
    ms = _half_sums(o * o, lo_s) * (1.0 / GDN_DV)
    y_ref[...] = o * lax.rsqrt(ms + NORM_EPS) * gn_ref[...] * _silu(z_ref[...])


def _gdn(qkv, z, lat, conv_w, alog_l, dtb_l, gn_l, b, s):
    npair = GDN_HEADS // GDN_PAIR
    nqk = GDN_HEADS * GDN_DK // LANE
    n_chunks = s // GDN_CHUNK
    unroll = math.gcd(n_chunks, GDN_UNROLL)
    seq_blk = lambda off: pl.BlockSpec((s, LANE), lambda bi, pi: (bi, off + pi))
    conv_blk = lambda off: pl.BlockSpec((CONV_WIDTH, LANE), lambda bi, pi: (0, off + pi))
    vec_blk = pl.BlockSpec((1, LANE), lambda bi, pi: (0, 0))
    return pl.pallas_call(
        functools.partial(_gdn_kernel, unroll=unroll),
        grid=(b, npair),
        in_specs=[seq_blk(0), seq_blk(nqk), seq_blk(2 * nqk),
                  pl.BlockSpec((s, LANE), lambda bi, pi: (bi, pi)),
                  pl.BlockSpec((s, LANE), lambda bi, pi: (bi, LAT_AB // LANE)),
                  conv_blk(0), conv_blk(nqk), conv_blk(2 * nqk),
                  vec_blk, vec_blk, vec_blk],
        out_specs=pl.BlockSpec((s, LANE), lambda bi, pi: (bi, pi)),
        out_shape=jax.ShapeDtypeStruct((b * s, GDN_HEADS * GDN_DV), F32),
        scratch_shapes=[pltpu.VMEM((s, LANE), F32)] * 5,
        compiler_params=_cparams(2),
        name="gdn_mixer",
    )(qkv, qkv, qkv, z, lat, conv_w, conv_w, conv_w, alog_l, dtb_l, gn_l)


def _out_ffn_kernel(x_ref, ym_ref, yg_ref, wo_ref, nf_ref, wg_ref, wu_ref, wd_ref, nfin_ref,
                    o_ref, *, final, ff_chunk):
    y = jnp.concatenate([ym_ref[...].astype(BF16), yg_ref[...].astype(BF16)], axis=1)
    x = x_ref[...] + jnp.dot(y, wo_ref[...], preferred_element_type=F32)
    hn = _rms(x, nf_ref[...]).astype(BF16)
    o_ref[...] = x
    for lo in range(0, D_FF, ff_chunk):
        sl = slice(lo, min(lo + ff_chunk, D_FF))
        g = jnp.dot(hn, wg_ref[:, sl], preferred_element_type=F32)
        u = jnp.dot(hn, wu_ref[:, sl], preferred_element_type=F32)
        o_ref[...] += jnp.dot((_silu(g) * u).astype(BF16), wd_ref[sl, :], preferred_element_type=F32)
    if final:
        o_ref[...] = _rms(o_ref[...], nfin_ref[...])


def _out_ffn(x, ym, yg, wo, nf, wg, wu, wd, nfin, tm, final):
    t = x.shape[0]
    const = lambda shape: pl.BlockSpec(shape, lambda i: (0, 0), pipeline_mode=pl.Buffered(1))
    return pl.pallas_call(
        functools.partial(_out_ffn_kernel, final=final, ff_chunk=FF_CHUNK),
        grid=(t // tm,),
        in_specs=[pl.BlockSpec((tm, D_MODEL), lambda i: (i, 0)),
                  pl.BlockSpec((tm, ym.shape[1]), lambda i: (i, 0)),
                  pl.BlockSpec((tm, yg.shape[1]), lambda i: (i, 0)),
                  const(wo.shape), const((1, D_MODEL)), const(wg.shape), const(wu.shape),
                  const(wd.shape), const((1, D_MODEL))],
        out_specs=pl.BlockSpec((tm, D_MODEL), lambda i: (i, 0)),
        out_shape=jax.ShapeDtypeStruct((t, D_MODEL), F32),
        compiler_params=_cparams(1),
        name="out_ffn",
    )(x, ym, yg, wo, nf, wg, wu, wd, nfin)


def _swap_halves(w):
    r = w.shape[-1] // 2
    return jnp.concatenate([w[..., r:], w[..., :r]], axis=-1)


def _prep_layer(w_in, w_q_up, w_kv_up):
    d = w_in.shape[0]
    o_kv = Q_LORA
    o_kr = o_kv + KV_LORA
    o_qkv = o_kr + MLA_ROPE
    o_z = o_qkv + QKV_W
    o_a = o_z + Z_W
    kr = w_in[:, o_kr:o_qkv]
    pad_rope = lambda w: jnp.concatenate(
        [jnp.zeros((d, MLA_NOPE), F32), w, jnp.zeros((d, HEAD_BLOCK - MLA_NOPE - MLA_ROPE), F32)], axis=1)
    ab = jnp.concatenate([w_in[:, o_a:], jnp.zeros((d, LANE - 4 * GDN_HEADS), F32)], axis=1)
    w_in_p = jnp.concatenate([w_in[:, :o_kr], pad_rope(kr), pad_rope(_swap_halves(kr)), ab,
                              w_in[:, o_qkv:o_z], w_in[:, o_z:o_a]], axis=1).astype(BF16)

    dq = MLA_NOPE + MLA_ROPE
    wq = w_q_up.reshape(Q_LORA, MLA_HEADS, dq)
    zpad = jnp.zeros((Q_LORA, MLA_HEADS, HEAD_BLOCK - dq), F32)
    wq_a = jnp.concatenate([wq, zpad], axis=-1)
    wq_b = jnp.concatenate([jnp.zeros((Q_LORA, MLA_HEADS, MLA_NOPE), F32),
                            _swap_halves(wq[..., MLA_NOPE:]), zpad], axis=-1)
    wq_p = jnp.concatenate([wq_a.reshape(Q_LORA, -1), wq_b.reshape(Q_LORA, -1)], axis=1).astype(BF16)

    wkv = w_kv_up.reshape(KV_LORA, MLA_HEADS, MLA_NOPE + MLA_V)
    wk = jnp.concatenate([wkv[..., :MLA_NOPE],
                          jnp.zeros((KV_LORA, MLA_HEADS, HEAD_BLOCK - MLA_NOPE), F32)], axis=-1)
    wk_p = wk.reshape(KV_LORA, -1).astype(BF16)
    wvt_p = wkv[..., MLA_NOPE:].reshape(KV_LORA, -1).T.astype(BF16)
    return w_in_p, wq_p, wk_p, wvt_p


def _rope_table(s):
    pos = jnp.arange(s, dtype=F32)
    inv = ROPE_THETA ** (-jnp.arange(0, MLA_ROPE, 2, dtype=F32) / MLA_ROPE)
    ang = pos[:, None] * inv[None, :]
    c, sn = jnp.cos(ang), jnp.sin(ang)
    cc = jnp.concatenate([c, c], axis=1)
    ss = jnp.concatenate([-sn, sn], axis=1)
    z_lo = jnp.zeros((s, MLA_NOPE), F32)
    z_hi = jnp.zeros((s, HEAD_BLOCK - MLA_NOPE - MLA_ROPE), F32)
    scale = (MLA_NOPE + MLA_ROPE) ** -0.5 * math.log2(math.e)
    ccq = jnp.concatenate([jnp.ones((s, MLA_NOPE), F32), cc, z_hi], axis=1) * scale
    ssq = jnp.concatenate([z_lo, ss, z_hi], axis=1) * scale
    cck = jnp.concatenate([z_lo, cc, z_hi], axis=1)
    ssk = jnp.concatenate([z_lo, ss, z_hi], axis=1)
    return jnp.concatenate([ccq, ssq, cck, ssk], axis=1)


def _lane_vec(v):
    v = v.reshape(1, -1).astype(F32)
    return jnp.concatenate([v, jnp.zeros((1, LANE - v.shape[1]), F32)], axis=1)


def _pick_tile(n, pref):
    t = min(pref, n)
    while n % t:
        t //= 2
    return t


def _trunk(x, layers, norm_final, tab):
    b, s, d = x.shape
    t = b * s
    tm = _pick_tile(s, 512)
    tq = _pick_tile(s, 256)
    kb = _pick_tile(s, 1024)
    h = x.reshape(t, d)
    depth = len(layers)
    for li, lw in enumerate(layers):
        lat, qkv, z = _in_proj(h, lw["norm_attn"], lw["w_in"], tm)
        q, k, v = _mla_proj(lat, tab, lw["q_norm"], lw["kv_norm"], lw["wq"], lw["wk"], lw["wvt"], b, s, tm)
        y_mla = _attention(q, k, v, tq, kb)
        y_gdn = _gdn(qkv, z, lat, lw["conv_w"], lw["a_log"], lw["dt_bias"], lw["gdn_norm"], b, s)
        h = _out_ffn(h, y_mla, y_gdn, lw["w_out"], lw["norm_ffn"], lw["w_gate"], lw["w_up"],
                     lw["w_down"], norm_final, tm, final=(li == depth - 1))
    return h.reshape(b, s, d)


def kernel(x_prompt, x_sample, norm_attn, w_in, q_norm, w_q_up, kv_norm, w_kv_up, conv_w, a_log, dt_bias, gdn_norm, w_out, norm_ffn, w_gate, w_up, w_down, norm_final):
    depth = w_in.shape[0]
    layers = []
    for l in range(depth):
        w_in_p, wq_p, wk_p, wvt_p = _prep_layer(w_in[l], w_q_up[l], w_kv_up[l])
        layers.append(dict(
            norm_attn=norm_attn[l].reshape(1, -1), w_in=w_in_p,
            q_norm=q_norm[l].reshape(1, -1), kv_norm=kv_norm[l].reshape(1, -1),
            wq=wq_p, wk=wk_p, wvt=wvt_p, conv_w=conv_w[l],
            a_log=_lane_vec(a_log[l]), dt_bias=_lane_vec(dt_bias[l]),
            gdn_norm=jnp.tile(gdn_norm[l].reshape(1, -1), (1, GDN_PAIR)),
            w_out=w_out[l].astype(BF16), norm_ffn=norm_ffn[l].reshape(1, -1),
            w_gate=w_gate[l].astype(BF16), w_up=w_up[l].astype(BF16),
            w_down=w_down[l].astype(BF16)))
    nfin = norm_final.reshape(1, -1)
    assert x_prompt.shape[1] == x_sample.shape[1]
    tab = _rope_table(x_prompt.shape[1])
    y_prompt = _trunk(x_prompt, layers, nfin, tab)
    y_sample = _trunk(x_sample, layers, nfin, tab)
    return (y_prompt, y_sample)
```

```python
import functools
import math

import jax
import jax.numpy as jnp
from jax import lax
from jax.experimental import pallas as pl
from jax.experimental.pallas import tpu as pltpu

F32 = jnp.float32
BF16 = jnp.bfloat16

D_MODEL = 1024
MLA_HEADS = 8
MLA_NOPE = 64
MLA_ROPE = 32
MLA_V = 64
Q_LORA = 256
KV_LORA = 128
ROPE_THETA = 10000.0
GDN_HEADS = 8
GDN_DK = 64
GDN_DV = 64
CONV_WIDTH = 3
D_FF = 2816
NORM_EPS = 1e-6

LANE = 128
HEAD_BLOCK = 128
GDN_CHUNK = 64
GDN_PAIR = 2
VMEM_LIMIT = 56 * 1024 * 1024
FF_CHUNK = 1536
GDN_UNROLL = 8

LAT_Q = 0
LAT_KV = Q_LORA
LAT_KR = LAT_KV + KV_LORA
LAT_KRS = LAT_KR + LANE
LAT_AB = LAT_KRS + LANE
LAT_W = LAT_AB + LANE
QKV_W = GDN_HEADS * (2 * GDN_DK + GDN_DV)
Z_W = GDN_HEADS * GDN_DV
IN_W = LAT_W + QKV_W + Z_W


def _dot(a, b):
    return jnp.dot(a.astype(BF16), b.astype(BF16), preferred_element_type=F32)


def _dot_nt(a, b):
    return lax.dot_general(a.astype(BF16), b.astype(BF16), (((1,), (1,)), ((), ())),
                           preferred_element_type=F32)


def _rms(x, g):
    return x * lax.rsqrt(jnp.mean(x * x, axis=-1, keepdims=True) + NORM_EPS) * g


def _silu(x):
    return x * jax.nn.sigmoid(x)


def _cparams(n_axes):
    return pltpu.CompilerParams(dimension_semantics=("arbitrary",) * n_axes,
                                vmem_limit_bytes=VMEM_LIMIT)


def _in_proj_kernel(x_ref, g_ref, w_ref, lat_ref, qkv_ref, z_ref):
    xn = _rms(x_ref[...], g_ref[...]).astype(BF16)
    lat_ref[...] = jnp.dot(xn, w_ref[:, :LAT_W], preferred_element_type=F32)
    qkv_ref[...] = jnp.dot(xn, w_ref[:, LAT_W:LAT_W + QKV_W], preferred_element_type=F32)
    z_ref[...] = jnp.dot(xn, w_ref[:, LAT_W + QKV_W:], preferred_element_type=F32)


def _in_proj(x, g, w, tm):
    t = x.shape[0]
    return pl.pallas_call(
        _in_proj_kernel,
        grid=(t // tm,),
        in_specs=[pl.BlockSpec((tm, D_MODEL), lambda i: (i, 0)),
                  pl.BlockSpec((1, D_MODEL), lambda i: (0, 0)),
                  pl.BlockSpec((D_MODEL, IN_W), lambda i: (0, 0))],
        out_specs=[pl.BlockSpec((tm, LAT_W), lambda i: (i, 0)),
                   pl.BlockSpec((tm, QKV_W), lambda i: (i, 0)),
                   pl.BlockSpec((tm, Z_W), lambda i: (i, 0))],
        out_shape=[jax.ShapeDtypeStruct((t, LAT_W), F32),
                   jax.ShapeDtypeStruct((t, QKV_W), F32),
                   jax.ShapeDtypeStruct((t, Z_W), F32)],
        compiler_params=_cparams(1),
        name="in_proj",
    )(x, g, w)


def _mla_proj_kernel(lat_ref, tab_ref, qn_ref, kvn_ref, wq_ref, wk_ref, wvt_ref, q_ref, k_ref, vt_ref):
    hw = MLA_HEADS * HEAD_BLOCK
    qn = _rms(lat_ref[:, LAT_Q:LAT_Q + Q_LORA], qn_ref[...]).astype(BF16)
    qa = jnp.dot(qn, wq_ref[:, :hw], preferred_element_type=F32)
    qb = jnp.dot(qn, wq_ref[:, hw:], preferred_element_type=F32)
    ccq = tab_ref[:, 0:LANE]
    ssq = tab_ref[:, LANE:2 * LANE]
    cck = tab_ref[:, 2 * LANE:3 * LANE]
    ssk = tab_ref[:, 3 * LANE:4 * LANE]
    kvn = _rms(lat_ref[:, LAT_KV:LAT_KV + KV_LORA], kvn_ref[...]).astype(BF16)
    ka = jnp.dot(kvn, wk_ref[...], preferred_element_type=F32)
    vt = lax.dot_general(wvt_ref[...], kvn, (((1,), (1,)), ((), ())), preferred_element_type=F32)
    kpe = lat_ref[:, LAT_KR:LAT_KR + LANE] * cck + lat_ref[:, LAT_KRS:LAT_KRS + LANE] * ssk
    for h in range(MLA_HEADS):
        sl = slice(h * HEAD_BLOCK, (h + 1) * HEAD_BLOCK)
        q_ref[0, h] = (qa[:, sl] * ccq + qb[:, sl] * ssq).astype(BF16)
        k_ref[0, h] = (ka[:, sl] + kpe).astype(BF16)
    for p in range(MLA_HEADS // 2):
        vt_ref[0, p] = vt[p * LANE:(p + 1) * LANE, :].astype(BF16)


def _mla_proj(lat, tab, qn, kvn, wq, wk, wvt, b, s, tm):
    nt = s // tm
    hw = MLA_HEADS * HEAD_BLOCK
    return pl.pallas_call(
        _mla_proj_kernel,
        grid=(b, nt),
        in_specs=[pl.BlockSpec((tm, LAT_AB), lambda bi, si: (bi * nt + si, 0)),
                  pl.BlockSpec((tm, 4 * LANE), lambda bi, si: (si, 0)),
                  pl.BlockSpec((1, Q_LORA), lambda bi, si: (0, 0)),
                  pl.BlockSpec((1, KV_LORA), lambda bi, si: (0, 0)),
                  pl.BlockSpec((Q_LORA, 2 * hw), lambda bi, si: (0, 0)),
                  pl.BlockSpec((KV_LORA, hw), lambda bi, si: (0, 0)),
                  pl.BlockSpec((MLA_HEADS * MLA_V, KV_LORA), lambda bi, si: (0, 0))],
        out_specs=[pl.BlockSpec((1, MLA_HEADS, tm, HEAD_BLOCK), lambda bi, si: (bi, 0, si, 0)),
                   pl.BlockSpec((1, MLA_HEADS, tm, HEAD_BLOCK), lambda bi, si: (bi, 0, si, 0)),
                   pl.BlockSpec((1, MLA_HEADS // 2, LANE, tm), lambda bi, si: (bi, 0, 0, si))],
        out_shape=[jax.ShapeDtypeStruct((b, MLA_HEADS, s, HEAD_BLOCK), BF16),
                   jax.ShapeDtypeStruct((b, MLA_HEADS, s, HEAD_BLOCK), BF16),
                   jax.ShapeDtypeStruct((b, MLA_HEADS // 2, LANE, s), BF16)],
        compiler_params=_cparams(2),
        name="mla_proj",
    )(lat, tab, qn, kvn, wq, wk, wvt)


def _attn_kernel(q_ref, k_ref, vt_ref, o_ref, *, kb):
    s = k_ref.shape[2]
    tq = q_ref.shape[2]
    units = [(h, j) for h in range(2) for j in range(s // kb)]

    def scores(unit):
        h, j = unit
        return lax.dot_general(k_ref[0, h, j * kb:(j + 1) * kb, :], q_ref[0, h],
                               (((1,), (1,)), ((), ())), preferred_element_type=F32)

    stats = [(jnp.full((1, tq), -jnp.inf, F32), jnp.zeros((1, tq), F32), jnp.zeros((MLA_V, tq), F32))] * 2
    sc_next = scores(units[0])
    for idx, (h, j) in enumerate(units):
        sc = sc_next
        if idx + 1 < len(units):
            sc_next = scores(units[idx + 1])
        m, l, acc = stats[h]
        m_new = jnp.maximum(m, jnp.max(sc, axis=0, keepdims=True))
        alpha = jnp.exp2(m - m_new)
        p = jnp.exp2(sc - m_new)
        l = alpha * l + jnp.sum(p, axis=0, keepdims=True)
        vt = vt_ref[0, 0, h * MLA_V:(h + 1) * MLA_V, j * kb:(j + 1) * kb]
        acc = alpha * acc + jnp.dot(vt, p.astype(BF16), preferred_element_type=F32)
        stats[h] = (m_new, l, acc)
    o_ref[...] = jnp.concatenate([acc / l for _, l, acc in stats], axis=0).T


def _attention(q, k, v, tq, kb):
    b, h, s, _ = q.shape
    nq = s // tq
    return pl.pallas_call(
        functools.partial(_attn_kernel, kb=kb),
        grid=(b, h // 2, nq),
        in_specs=[pl.BlockSpec((1, 2, tq, HEAD_BLOCK), lambda bi, pi, qi: (bi, pi, qi, 0)),
                  pl.BlockSpec((1, 2, s, HEAD_BLOCK), lambda bi, pi, qi: (bi, pi, 0, 0)),
                  pl.BlockSpec((1, 1, LANE, s), lambda bi, pi, qi: (bi, pi, 0, 0))],
        out_specs=pl.BlockSpec((tq, LANE), lambda bi, pi, qi: (bi * nq + qi, pi)),
        out_shape=jax.ShapeDtypeStruct((b * s, h * MLA_V), F32),
        compiler_params=_cparams(3),
        name="mla_attention",
    )(q, k, v)


def _half_sums(x, lane_lo):
    s0 = jnp.sum(jnp.where(lane_lo, x, 0.0), axis=-1, keepdims=True)
    s1 = jnp.sum(jnp.where(lane_lo, 0.0, x), axis=-1, keepdims=True)
    return jnp.where(lane_lo, s0, s1)


def _pair(a, b):
    return jnp.concatenate([a, b], axis=1)


def _block_diag(x2):
    w = x2.shape[1] // 2
    z = jnp.zeros((x2.shape[0], w), x2.dtype)
    return jnp.concatenate([jnp.concatenate([x2[:, :w], z], axis=1),
                            jnp.concatenate([z, x2[:, w:]], axis=1)], axis=0)


def _dot_pair(a2, b2):
    return jnp.dot(a2.astype(BF16), _block_diag(b2.astype(BF16)), preferred_element_type=F32)


def _gdn_kernel(q_ref, k_ref, v_ref, z_ref, ab_ref, cq_ref, ck_ref, cv_ref, alog_ref, dtb_ref,
                gn_ref, y_ref, qs, ks, vs, of_s, ob_s, *, unroll):
    s = q_ref.shape[0]
    c = GDN_CHUNK
    n = s // c
    c2 = 2 * c
    pid = pl.program_id(1)

    row_s = lax.broadcasted_iota(jnp.int32, (s, LANE), 0)
    lane_s = lax.broadcasted_iota(jnp.int32, (s, LANE), 1)
    lo_s = lane_s < GDN_DK

    def conv_silu(x_ref, w_ref):
        x = x_ref[...]
        prev = jnp.where(row_s == 0, 0.0, pltpu.roll(x, 1, 0))
        nxt = jnp.where(row_s == s - 1, 0.0, pltpu.roll(x, s - 1, 0))
        return _silu(prev * w_ref[0:1, :] + x * w_ref[1:2, :] + nxt * w_ref[2:3, :])

    def l2n(x):
        return x * lax.rsqrt(_half_sums(x * x, lo_s) + NORM_EPS)

    qs[...] = l2n(conv_silu(q_ref, cq_ref)) * (GDN_DK ** -0.5)
    ks[...] = l2n(conv_silu(k_ref, ck_ref))
    vs[...] = conv_silu(v_ref, cv_ref)

    ri = lax.broadcasted_iota(jnp.int32, (c2, 2 * c2), 0)
    cj = lax.broadcasted_iota(jnp.int32, (c2, 2 * c2), 1)
    is_b = cj >= c2
    cc = jnp.where(is_b, cj - c2, cj)
    same_head = (ri // c) == (cc // c)
    incl = same_head & ((is_b & (ri <= cc)) | (jnp.logical_not(is_b) & (ri >= cc)))
    strict = incl & (ri != cc)
    eye = (ri == cc).astype(F32)
    tri = incl.astype(BF16)
    top = ri < c
    top1 = top[:, :c2]
    lane_c = lax.broadcasted_iota(jnp.int32, (c, LANE), 1)
    lo_c = lane_c < GDN_DK
    zero_bf = jnp.zeros((c2, c2), BF16)

    def stack(x):
        return jnp.concatenate([jnp.where(lo_c, x, 0.0), jnp.where(lo_c, 0.0, x)], axis=0)

    def col(x, lane_idx):
        return jnp.sum(jnp.where(lane_c == lane_idx, x, 0.0), axis=-1, keepdims=True)

    def gates(chunk, d):
        ab = ab_ref[pl.ds(pl.multiple_of(chunk * c, c), c), :]
        t = ab + dtb_ref[...]
        softplus = jnp.maximum(t, 0.0) + jnp.log(1.0 + jnp.exp(-jnp.abs(t)))
        g16 = -jnp.exp(alog_ref[...]) * softplus
        b16 = jax.nn.sigmoid(ab)
        h0 = d * GDN_HEADS + GDN_PAIR * pid
        bc = lambda x, i: jnp.broadcast_to(col(x, i), (c, LANE))
        g = jnp.concatenate([bc(g16, h0), bc(g16, h0 + 1)], axis=0)
        beta = jnp.concatenate([bc(b16, 2 * GDN_HEADS + h0), bc(b16, 2 * GDN_HEADS + h0 + 1)], axis=0)
        return g, beta

    def prep(cf, cb):
        rf = pl.multiple_of(cf * c, c)
        rb = pl.multiple_of(cb * c, c)
        k_f, k_b = stack(ks[pl.ds(rf, c), :]), stack(ks[pl.ds(rb, c), :])
        q_f, q_b = stack(qs[pl.ds(rf, c), :]), stack(qs[pl.ds(rb, c), :])
        k2 = _pair(k_f, k_b)
        q2 = _pair(q_f, q_b)
        v2 = _pair(stack(vs[pl.ds(rf, c), :]), stack(vs[pl.ds(rb, c), :]))
        g_f, beta_f = gates(cf, 0)
        g_b, beta_b = gates(cb, 1)
        beta2 = _pair(beta_f, beta_b)
        ghl = []
        for g in (g_f, g_b):
            hi = g.astype(BF16)
            ghl.append((hi, (g - hi.astype(F32)).astype(BF16)))
        wcum = jnp.concatenate(
            [jnp.concatenate([ghl[0][0], ghl[0][1], zero_bf, zero_bf], axis=1),
             jnp.concatenate([zero_bf, zero_bf, ghl[1][0], ghl[1][1]], axis=1)], axis=0)
        cum = jnp.dot(tri, wcum, preferred_element_type=F32)
        gram = _dot_nt(jnp.concatenate([k2, q2], axis=0), _block_diag(k2))
        yield
        gc_f = cum[:, 0:c2] + cum[:, c2:2 * c2]
        gc_b = cum[:, 2 * c2:3 * c2] + cum[:, 3 * c2:4 * c2]
        gc2 = _pair(gc_f, gc_b)
        dlog = _pair(gc_f - gc_f.T, gc_b - gc_b.T)
        decay = jnp.where(incl, jnp.exp(jnp.where(incl, dlog, 0.0)), 0.0)
        eg = jnp.exp(gc2)
        gl2 = _pair(jnp.where(top1, gc_f[c - 1:c, :], gc_f[c2 - 1:c2, :]),
                    jnp.where(top1, gc_b[0:1, :], gc_b[c:c + 1, :]))
        kk = gram[:c2]
        qk = gram[c2:]
        mm = jnp.where(strict, kk * decay * beta2, 0.0)
        attn = qk * decay
        tinv = eye - jnp.where((ri // 2) == (cc // 2), mm, 0.0)
        sz = 2
        while sz < c:
            nmask = ((ri // (2 * sz)) == (cc // (2 * sz))) & ((ri // sz) != (cc // sz))
            nt = _dot_pair(jnp.where(nmask, mm, 0.0), tinv)
            yield
            tinv = tinv - _dot_pair(tinv, nt)
            yield
            sz *= 2
        vb = (v2 * beta2).astype(BF16)
        kbe = (k2 * beta2 * eg).astype(BF16)
        rhs = jnp.concatenate(
            [jnp.concatenate([vb[:, :c2], kbe[:, :c2], zero_bf, zero_bf], axis=1),
             jnp.concatenate([zero_bf, zero_bf, vb[:, c2:], kbe[:, c2:]], axis=1)], axis=0)
        uw = jnp.dot(tinv.astype(BF16), rhs, preferred_element_type=F32).astype(BF16)
        yield
        kg = k2 * jnp.exp(gl2 - gc2)
        lhs = jnp.concatenate([_pair(kg[:, :c2].T, kg[:, c2:].T), attn], axis=0).astype(BF16)
        zz = jnp.zeros((c2, 2 * c2), BF16)
        res = jnp.dot(lhs, jnp.concatenate([jnp.concatenate([uw[:, :2 * c2], zz], axis=1),
                                            jnp.concatenate([zz, uw[:, 2 * c2:]], axis=1)], axis=0),
                      preferred_element_type=F32)
        yield
        b2 = _pair(res[:c2, 0:c2], res[:c2, 2 * c2:3 * c2])
        wk = _pair(res[:c2, c2:2 * c2], res[:c2, 3 * c2:4 * c2])
        o2 = _pair(res[c2:, 0:c2], res[c2:, 2 * c2:3 * c2])
        aw = _pair(res[c2:, c2:2 * c2], res[c2:, 3 * c2:4 * c2])
        qp = q2 * eg - aw
        return jnp.concatenate([wk, qp], axis=0).astype(BF16), b2, o2, jnp.exp(gl2)

    def scan(base, state, pres):
        for j, (lhs, b2, o2, egl) in enumerate(pres):
            cf = base + j
            cb = n - 1 - cf
            r = jnp.dot(lhs, _block_diag(state.astype(BF16)), preferred_element_type=F32)
            out = r[c2:] + o2
            state = state * egl + b2 - r[:c2]
            of_s[pl.ds(pl.multiple_of(cf * c, c), c), :] = out[:c, :c2] + out[c:, :c2]
            ob_s[pl.ds(pl.multiple_of(cb * c, c), c), :] = out[:c, c2:] + out[c:, c2:]
            yield
        return state

    def run_lockstep(gens):
        results = [None] * len(gens)
        live = list(range(len(gens)))
        while live:
            for i in list(live):
                try:
                    next(gens[i])
                except StopIteration as stop:
                    results[i] = stop.value
                    live.remove(i)
        return results

    n_groups = n // unroll

    def preps(group):
        base = group * unroll
        return [prep(base + j, n - 1 - (base + j)) for j in range(unroll)]

    def body(g, carry):
        state, pres = carry
        out = run_lockstep(preps(g) + [scan((g - 1) * unroll, state, pres)])
        return out[-1], out[:-1]

    state, pres = lax.fori_loop(1, n_groups, body,
                                (jnp.zeros((c2, 2 * c2), F32), run_lockstep(preps(0))))
    run_lockstep([scan((n_groups - 1) * unroll, state, pres)])

    o = of_s[...] + ob_s[...]
    ms = _half_sums(o * o, lo_s) * (1.0 / GDN_DV)
    y_ref[...] = o * lax.rsqrt(ms + NORM_EPS) * gn_ref[...] * _silu(z_ref[...])


def _gdn(qkv, z, lat, conv_w, alog_l, dtb_l, gn_l, b, s):
    npair = GDN_HEADS // GDN_PAIR
    nqk = GDN_HEADS * GDN_DK // LANE
    n_chunks = s // GDN_CHUNK
    unroll = math.gcd(n_chunks, GDN_UNROLL)
    seq_blk = lambda off: pl.BlockSpec((s, LANE), lambda bi, pi: (bi, off + pi))
    conv_blk = lambda off: pl.BlockSpec((CONV_WIDTH, LANE), lambda bi, pi: (0, off + pi))
    vec_blk = pl.BlockSpec((1, LANE), lambda bi, pi: (0, 0))
    return pl.pallas_call(
        functools.partial(_gdn_kernel, unroll=unroll),
        grid=(b, npair),
        in_specs=[seq_blk(0), seq_blk(nqk), seq_blk(2 * nqk),
                  pl.BlockSpec((s, LANE), lambda bi, pi: (bi, pi)),
                  pl.BlockSpec((s, LANE), lambda bi, pi: (bi, LAT_AB // LANE)),
                  conv_blk(0), conv_blk(nqk), conv_blk(2 * nqk),
                  vec_blk, vec_blk, vec_blk],
        out_specs=pl.BlockSpec((s, LANE), lambda bi, pi: (bi, pi)),
        out_shape=jax.ShapeDtypeStruct((b * s, GDN_HEADS * GDN_DV), F32),
        scratch_shapes=[pltpu.VMEM((s, LANE), F32)] * 5,
        compiler_params=_cparams(2),
        name="gdn_mixer",
    )(qkv, qkv, qkv, z, lat, conv_w, conv_w, conv_w, alog_l, dtb_l, gn_l)


def _out_ffn_kernel(x_ref, ym_ref, yg_ref, wo_ref, nf_ref, wg_ref, wu_ref, wd_ref, nfin_ref,
                    o_ref, *, final, ff_chunk):
    y = jnp.concatenate([ym_ref[...].astype(BF16), yg_ref[...].astype(BF16)], axis=1)
    x = x_ref[...] + jnp.dot(y, wo_ref[...], preferred_element_type=F32)
    hn = _rms(x, nf_ref[...]).astype(BF16)
    o_ref[...] = x
    for lo in range(0, D_FF, ff_chunk):
        sl = slice(lo, min(lo + ff_chunk, D_FF))
        g = jnp.dot(hn, wg_ref[:, sl], preferred_element_type=F32)
        u = jnp.dot(hn, wu_ref[:, sl], preferred_element_type=F32)
        o_ref[...] += jnp.dot((_silu(g) * u).astype(BF16), wd_ref[sl, :], preferred_element_type=F32)
    if final:
        o_ref[...] = _rms(o_ref[...], nfin_ref[...])


def _out_ffn(x, ym, yg, wo, nf, wg, wu, wd, nfin, tm, final):
    t = x.shape[0]
    const = lambda shape: pl.BlockSpec(shape, lambda i: (0, 0), pipeline_mode=pl.Buffered(1))
    return pl.pallas_call(
        functools.partial(_out_ffn_kernel, final=final, ff_chunk=FF_CHUNK),
        grid=(t // tm,),
        in_specs=[pl.BlockSpec((tm, D_MODEL), lambda i: (i, 0)),
                  pl.BlockSpec((tm, ym.shape[1]), lambda i: (i, 0)),
                  pl.BlockSpec((tm, yg.shape[1]), lambda i: (i, 0)),
                  const(wo.shape), const((1, D_MODEL)), const(wg.shape), const(wu.shape),
                  const(wd.shape), const((1, D_MODEL))],
        out_specs=pl.BlockSpec((tm, D_MODEL), lambda i: (i, 0)),
        out_shape=jax.ShapeDtypeStruct((t, D_MODEL), F32),
        compiler_params=_cparams(1),
        name="out_ffn",
    )(x, ym, yg, wo, nf, wg, wu, wd, nfin)


def _swap_halves(w):
    r = w.shape[-1] // 2
    return jnp.concatenate([w[..., r:], w[..., :r]], axis=-1)


def _prep_layer(w_in, w_q_up, w_kv_up):
    d = w_in.shape[0]
    o_kv = Q_LORA
    o_kr = o_kv + KV_LORA
    o_qkv = o_kr + MLA_ROPE
    o_z = o_qkv + QKV_W
    o_a = o_z + Z_W
    kr = w_in[:, o_kr:o_qkv]
    pad_rope = lambda w: jnp.concatenate(
        [jnp.zeros((d, MLA_NOPE), F32), w, jnp.zeros((d, HEAD_BLOCK - MLA_NOPE - MLA_ROPE), F32)], axis=1)
    ab = jnp.concatenate([w_in[:, o_a:], jnp.zeros((d, LANE - 4 * GDN_HEADS), F32)], axis=1)
    w_in_p = jnp.concatenate([w_in[:, :o_kr], pad_rope(kr), pad_rope(_swap_halves(kr)), ab,
                              w_in[:, o_qkv:o_z], w_in[:, o_z:o_a]], axis=1).astype(BF16)

    dq = MLA_NOPE + MLA_ROPE
    wq = w_q_up.reshape(Q_LORA, MLA_HEADS, dq)
    zpad = jnp.zeros((Q_LORA, MLA_HEADS, HEAD_BLOCK - dq), F32)
    wq_a = jnp.concatenate([wq, zpad], axis=-1)
    wq_b = jnp.concatenate([jnp.zeros((Q_LORA, MLA_HEADS, MLA_NOPE), F32),
                            _swap_halves(wq[..., MLA_NOPE:]), zpad], axis=-1)
    wq_p = jnp.concatenate([wq_a.reshape(Q_LORA, -1), wq_b.reshape(Q_LORA, -1)], axis=1).astype(BF16)

    wkv = w_kv_up.reshape(KV_LORA, MLA_HEADS, MLA_NOPE + MLA_V)
    wk = jnp.concatenate([wkv[..., :MLA_NOPE],
                          jnp.zeros((KV_LORA, MLA_HEADS, HEAD_BLOCK - MLA_NOPE), F32)], axis=-1)
    wk_p = wk.reshape(KV_LORA, -1).astype(BF16)
    wvt_p = wkv[..., MLA_NOPE:].reshape(KV_LORA, -1).T.astype(BF16)
    return w_in_p, wq_p, wk_p, wvt_p


def _rope_table(s):
    pos = jnp.arange(s, dtype=F32)
    inv = ROPE_THETA ** (-jnp.arange(0, MLA_ROPE, 2, dtype=F32) / MLA_ROPE)
    ang = pos[:, None] * inv[None, :]
    c, sn = jnp.cos(ang), jnp.sin(ang)
    cc = jnp.concatenate([c, c], axis=1)
    ss = jnp.concatenate([-sn, sn], axis=1)
    z_lo = jnp.zeros((s, MLA_NOPE), F32)
    z_hi = jnp.zeros((s, HEAD_BLOCK - MLA_NOPE - MLA_ROPE), F32)
    scale = (MLA_NOPE + MLA_ROPE) ** -0.5 * math.log2(math.e)
    ccq = jnp.concatenate([jnp.ones((s, MLA_NOPE), F32), cc, z_hi], axis=1) * scale
    ssq = jnp.concatenate([z_lo, ss, z_hi], axis=1) * scale
    cck = jnp.concatenate([z_lo, cc, z_hi], axis=1)
    ssk = jnp.concatenate([z_lo, ss, z_hi], axis=1)
    return jnp.concatenate([ccq, ssq, cck, ssk], axis=1)


def _lane_vec(v):
    v = v.reshape(1, -1).astype(F32)
    return jnp.concatenate([v, jnp.zeros((1, LANE - v.shape[1]), F32)], axis=1)


def _pick_tile(n, pref):
    t = min(pref, n)
    while n % t:
        t //= 2
    return t


def _trunk(x, layers, norm_final, tab):
    b, s, d = x.shape
    t = b * s
    tm = _pick_tile(s, 512)
    tq = _pick_tile(s, 512)
    kb = _pick_tile(s, 1024)
    h = x.reshape(t, d)
    depth = len(layers)
    for li, lw in enumerate(layers):
        lat, qkv, z = _in_proj(h, lw["norm_attn"], lw["w_in"], tm)
        q, k, v = _mla_proj(lat, tab, lw["q_norm"], lw["kv_norm"], lw["wq"], lw["wk"], lw["wvt"], b, s, tm)
        y_mla = _attention(q, k, v, tq, kb)
        y_gdn = _gdn(qkv, z, lat, lw["conv_w"], lw["a_log"], lw["dt_bias"], lw["gdn_norm"], b, s)
        h = _out_ffn(h, y_mla, y_gdn, lw["w_out"], lw["norm_ffn"], lw["w_gate"], lw["w_up"],
                     lw["w_down"], norm_final, tm, final=(li == depth - 1))
    return h.reshape(b, s, d)


def kernel(x_prompt, x_sample, norm_attn, w_in, q_norm, w_q_up, kv_norm, w_kv_up, conv_w, a_log, dt_bias, gdn_norm, w_out, norm_ffn, w_gate, w_up, w_down, norm_final):
    depth = w_in.shape[0]
    layers = []
    for l in range(depth):
        w_in_p, wq_p, wk_p, wvt_p = _prep_layer(w_in[l], w_q_up[l], w_kv_up[l])
        layers.append(dict(
            norm_attn=norm_attn[l].reshape(1, -1), w_in=w_in_p,
            q_norm=q_norm[l].reshape(1, -1), kv_norm=kv_norm[l].reshape(1, -1),
            wq=wq_p, wk=wk_p, wvt=wvt_p, conv_w=conv_w[l],
            a_log=_lane_vec(a_log[l]), dt_bias=_lane_vec(dt_bias[l]),
            gdn_norm=jnp.tile(gdn_norm[l].reshape(1, -1), (1, GDN_PAIR)),
            w_out=w_out[l].astype(BF16), norm_ffn=norm_ffn[l].reshape(1, -1),
            w_gate=w_gate[l].astype(BF16), w_up=w_up[l].astype(BF16),
            w_down=w_down[l].astype(BF16)))
    nfin = norm_final.reshape(1, -1)
    assert x_prompt.shape[1] == x_sample.shape[1]
    tab = _rope_table(x_prompt.shape[1])
    y_prompt = _trunk(x_prompt, layers, nfin, tab)
    y_sample = _trunk(x_sample, layers, nfin, tab)
    return (y_prompt, y_sample)
```

```python
import functools
import math

import jax
import jax.numpy as jnp
from jax import lax
from jax.experimental import pallas as pl
from jax.experimental.pallas import tpu as pltpu

F32 = jnp.float32
BF16 = jnp.bfloat16

D_MODEL = 1024
MLA_HEADS = 8
MLA_NOPE = 64
MLA_ROPE = 32
MLA_V = 64
Q_LORA = 256
KV_LORA = 128
ROPE_THETA = 10000.0
GDN_HEADS = 8
GDN_DK = 64
GDN_DV = 64
CONV_WIDTH = 3
D_FF = 2816
NORM_EPS = 1e-6

LANE = 128
HEAD_BLOCK = 128
GDN_CHUNK = 64
GDN_PAIR = 2
VMEM_LIMIT = 56 * 1024 * 1024
FF_CHUNK = 1536
GDN_UNROLL = 8

LAT_Q = 0
LAT_KV = Q_LORA
LAT_KR = LAT_KV + KV_LORA
LAT_KRS = LAT_KR + LANE
LAT_AB = LAT_KRS + LANE
LAT_W = LAT_AB + LANE
QKV_W = GDN_HEADS * (2 * GDN_DK + GDN_DV)
Z_W = GDN_HEADS * GDN_DV
IN_W = LAT_W + QKV_W + Z_W


def _dot(a, b):
    return jnp.dot(a.astype(BF16), b.astype(BF16), preferred_element_type=F32)


def _dot_nt(a, b):
    return lax.dot_general(a.astype(BF16), b.astype(BF16), (((1,), (1,)), ((), ())),
                           preferred_element_type=F32)


def _rms(x, g):
    return x * lax.rsqrt(jnp.mean(x * x, axis=-1, keepdims=True) + NORM_EPS) * g


def _silu(x):
    return x * jax.nn.sigmoid(x)


def _cparams(n_axes):
    return pltpu.CompilerParams(dimension_semantics=("arbitrary",) * n_axes,
                                vmem_limit_bytes=VMEM_LIMIT)


def _in_proj_kernel(x_ref, g_ref, w_ref, lat_ref, qkv_ref, z_ref):
    xn = _rms(x_ref[...], g_ref[...]).astype(BF16)
    lat_ref[...] = jnp.dot(xn, w_ref[:, :LAT_W], preferred_element_type=F32)
    qkv_ref[...] = jnp.dot(xn, w_ref[:, LAT_W:LAT_W + QKV_W], preferred_element_type=F32)
    z_ref[...] = jnp.dot(xn, w_ref[:, LAT_W + QKV_W:], preferred_element_type=F32)


def _in_proj(x, g, w, tm):
    t = x.shape[0]
    return pl.pallas_call(
        _in_proj_kernel,
        grid=(t // tm,),
        in_specs=[pl.BlockSpec((tm, D_MODEL), lambda i: (i, 0)),
                  pl.BlockSpec((1, D_MODEL), lambda i: (0, 0)),
                  pl.BlockSpec((D_MODEL, IN_W), lambda i: (0, 0))],
        out_specs=[pl.BlockSpec((tm, LAT_W), lambda i: (i, 0)),
                   pl.BlockSpec((tm, QKV_W), lambda i: (i, 0)),
                   pl.BlockSpec((tm, Z_W), lambda i: (i, 0))],
        out_shape=[jax.ShapeDtypeStruct((t, LAT_W), F32),
                   jax.ShapeDtypeStruct((t, QKV_W), F32),
                   jax.ShapeDtypeStruct((t, Z_W), F32)],
        compiler_params=_cparams(1),
        name="in_proj",
    )(x, g, w)


def _mla_proj_kernel(lat_ref, tab_ref, qn_ref, kvn_ref, wq_ref, wk_ref, wvt_ref, q_ref, k_ref, vt_ref):
    hw = MLA_HEADS * HEAD_BLOCK
    qn = _rms(lat_ref[:, LAT_Q:LAT_Q + Q_LORA], qn_ref[...]).astype(BF16)
    qa = jnp.dot(qn, wq_ref[:, :hw], preferred_element_type=F32)
    qb = jnp.dot(qn, wq_ref[:, hw:], preferred_element_type=F32)
    ccq = tab_ref[:, 0:LANE]
    ssq = tab_ref[:, LANE:2 * LANE]
    cck = tab_ref[:, 2 * LANE:3 * LANE]
    ssk = tab_ref[:, 3 * LANE:4 * LANE]
    kvn = _rms(lat_ref[:, LAT_KV:LAT_KV + KV_LORA], kvn_ref[...]).astype(BF16)
    ka = jnp.dot(kvn, wk_ref[...], preferred_element_type=F32)
    vt = lax.dot_general(wvt_ref[...], kvn, (((1,), (1,)), ((), ())), preferred_element_type=F32)
    kpe = lat_ref[:, LAT_KR:LAT_KR + LANE] * cck + lat_ref[:, LAT_KRS:LAT_KRS + LANE] * ssk
    for h in range(MLA_HEADS):
        sl = slice(h * HEAD_BLOCK, (h + 1) * HEAD_BLOCK)
        q_ref[0, h] = (qa[:, sl] * ccq + qb[:, sl] * ssq).astype(BF16)
        k_ref[0, h] = (ka[:, sl] + kpe).astype(BF16)
    for p in range(MLA_HEADS // 2):
        vt_ref[0, p] = vt[p * LANE:(p + 1) * LANE, :].astype(BF16)


def _mla_proj(lat, tab, qn, kvn, wq, wk, wvt, b, s, tm):
    nt = s // tm
    hw = MLA_HEADS * HEAD_BLOCK
    return pl.pallas_call(
        _mla_proj_kernel,
        grid=(b, nt),
        in_specs=[pl.BlockSpec((tm, LAT_AB), lambda bi, si: (bi * nt + si, 0)),
                  pl.BlockSpec((tm, 4 * LANE), lambda bi, si: (si, 0)),
                  pl.BlockSpec((1, Q_LORA), lambda bi, si: (0, 0)),
                  pl.BlockSpec((1, KV_LORA), lambda bi, si: (0, 0)),
                  pl.BlockSpec((Q_LORA, 2 * hw), lambda bi, si: (0, 0)),
                  pl.BlockSpec((KV_LORA, hw), lambda bi, si: (0, 0)),
                  pl.BlockSpec((MLA_HEADS * MLA_V, KV_LORA), lambda bi, si: (0, 0))],
        out_specs=[pl.BlockSpec((1, MLA_HEADS, tm, HEAD_BLOCK), lambda bi, si: (bi, 0, si, 0)),
                   pl.BlockSpec((1, MLA_HEADS, tm, HEAD_BLOCK), lambda bi, si: (bi, 0, si, 0)),
                   pl.BlockSpec((1, MLA_HEADS // 2, LANE, tm), lambda bi, si: (bi, 0, 0, si))],
        out_shape=[jax.ShapeDtypeStruct((b, MLA_HEADS, s, HEAD_BLOCK), BF16),
                   jax.ShapeDtypeStruct((b, MLA_HEADS, s, HEAD_BLOCK), BF16),
                   jax.ShapeDtypeStruct((b, MLA_HEADS // 2, LANE, s), BF16)],
        compiler_params=_cparams(2),
        name="mla_proj",
    )(lat, tab, qn, kvn, wq, wk, wvt)


def _attn_kernel(q_ref, k_ref, vt_ref, o_ref, *, kb):
    s = k_ref.shape[2]
    tq = q_ref.shape[2]
    units = [(h, j) for h in range(2) for j in range(s // kb)]

    def scores(unit):
        h, j = unit
        return lax.dot_general(k_ref[0, h, j * kb:(j + 1) * kb, :], q_ref[0, h],
                               (((1,), (1,)), ((), ())), preferred_element_type=F32)

    stats = [(jnp.full((1, tq), -jnp.inf, F32), jnp.zeros((1, tq), F32), jnp.zeros((MLA_V, tq), F32))] * 2
    sc_next = scores(units[0])
    for idx, (h, j) in enumerate(units):
        sc = sc_next
        if idx + 1 < len(units):
            sc_next = scores(units[idx + 1])
        m, l, acc = stats[h]
        m_new = jnp.maximum(m, jnp.max(sc, axis=0, keepdims=True))
        alpha = jnp.exp2(m - m_new)
        p = jnp.exp2(sc - m_new)
        l = alpha * l + jnp.sum(p, axis=0, keepdims=True)
        vt = vt_ref[0, 0, h * MLA_V:(h + 1) * MLA_V, j * kb:(j + 1) * kb]
        acc = alpha * acc + jnp.dot(vt, p.astype(BF16), preferred_element_type=F32)
        stats[h] = (m_new, l, acc)
    o_ref[...] = jnp.concatenate([acc / l for _, l, acc in stats], axis=0).T


def _attention(q, k, v, tq, kb):
    b, h, s, _ = q.shape
    nq = s // tq
    return pl.pallas_call(
        functools.partial(_attn_kernel, kb=kb),
        grid=(b, h // 2, nq),
        in_specs=[pl.BlockSpec((1, 2, tq, HEAD_BLOCK), lambda bi, pi, qi: (bi, pi, qi, 0)),
                  pl.BlockSpec((1, 2, s, HEAD_BLOCK), lambda bi, pi, qi: (bi, pi, 0, 0)),
                  pl.BlockSpec((1, 1, LANE, s), lambda bi, pi, qi: (bi, pi, 0, 0))],
        out_specs=pl.BlockSpec((tq, LANE), lambda bi, pi, qi: (bi * nq + qi, pi)),
        out_shape=jax.ShapeDtypeStruct((b * s, h * MLA_V), F32),
        compiler_params=_cparams(3),
        name="mla_attention",
    )(q, k, v)


def _half_sums(x, lane_lo):
    s0 = jnp.sum(jnp.where(lane_lo, x, 0.0), axis=-1, keepdims=True)
    s1 = jnp.sum(jnp.where(lane_lo, 0.0, x), axis=-1, keepdims=True)
    return jnp.where(lane_lo, s0, s1)


def _pair(a, b):
    return jnp.concatenate([a, b], axis=1)


def _block_diag(x2):
    w = x2.shape[1] // 2
    z = jnp.zeros((x2.shape[0], w), x2.dtype)
    return jnp.concatenate([jnp.concatenate([x2[:, :w], z], axis=1),
                            jnp.concatenate([z, x2[:, w:]], axis=1)], axis=0)


def _dot_pair(a2, b2):
    return jnp.dot(a2.astype(BF16), _block_diag(b2.astype(BF16)), preferred_element_type=F32)


def _gdn_kernel(q_ref, k_ref, v_ref, z_ref, ab_ref, cq_ref, ck_ref, cv_ref, alog_ref, dtb_ref,
                gn_ref, y_ref, qs, ks, vs, of_s, ob_s, *, unroll):
    s = q_ref.shape[0]
    c = GDN_CHUNK
    n = s // c
    c2 = 2 * c
    pid = pl.program_id(1)

    def conv_silu(x_ref, w_ref, r0, rows):
        zrow = jnp.zeros((1, LANE), F32)
        x = x_ref[r0:r0 + rows, :]
        prev = (jnp.concatenate([zrow, x_ref[0:rows - 1, :]], axis=0) if r0 == 0
                else x_ref[r0 - 1:r0 - 1 + rows, :])
        nxt = (jnp.concatenate([x_ref[r0 + 1:s, :], zrow], axis=0) if r0 + rows == s
               else x_ref[r0 + 1:r0 + 1 + rows, :])
        return _silu(prev * w_ref[0:1, :] + x * w_ref[1:2, :] + nxt * w_ref[2:3, :])

    def l2n(x):
        lo = lax.broadcasted_iota(jnp.int32, x.shape, 1) < GDN_DK
        return x * lax.rsqrt(_half_sums(x * x, lo) + NORM_EPS)

    def prepare_rows(r0, rows):
        qs[r0:r0 + rows, :] = l2n(conv_silu(q_ref, cq_ref, r0, rows)) * (GDN_DK ** -0.5)
        ks[r0:r0 + rows, :] = l2n(conv_silu(k_ref, ck_ref, r0, rows))
        vs[r0:r0 + rows, :] = conv_silu(v_ref, cv_ref, r0, rows)

    def finish_rows(r0, rows):
        o = of_s[r0:r0 + rows, :] + ob_s[r0:r0 + rows, :]
        lo = lax.broadcasted_iota(jnp.int32, o.shape, 1) < GDN_DV
        ms = _half_sums(o * o, lo) * (1.0 / GDN_DV)
        y_ref[r0:r0 + rows, :] = (o * lax.rsqrt(ms + NORM_EPS) * gn_ref[...]
                                  * _silu(z_ref[r0:r0 + rows, :]))

    def rows_gen(fn, blocks):
        for r0 in blocks:
            fn(r0, group_rows)
            yield

    group_rows = unroll * c
    edge_blocks = sorted({0, s - group_rows})
    mid_blocks = [r for r in range(0, s, group_rows) if r not in edge_blocks]
    for r0 in edge_blocks:
        prepare_rows(r0, group_rows)

    ri = lax.broadcasted_iota(jnp.int32, (c2, 2 * c2), 0)
    cj = lax.broadcasted_iota(jnp.int32, (c2, 2 * c2), 1)
    is_b = cj >= c2
    cc = jnp.where(is_b, cj - c2, cj)
    same_head = (ri // c) == (cc // c)
    incl = same_head & ((is_b & (ri <= cc)) | (jnp.logical_not(is_b) & (ri >= cc)))
    strict = incl & (ri != cc)
    eye = (ri == cc).astype(F32)
    tri = incl.astype(BF16)
    top = ri < c
    top1 = top[:, :c2]
    lane_c = lax.broadcasted_iota(jnp.int32, (c, LANE), 1)
    lo_c = lane_c < GDN_DK
    zero_bf = jnp.zeros((c2, c2), BF16)

    def stack(x):
        return jnp.concatenate([jnp.where(lo_c, x, 0.0), jnp.where(lo_c, 0.0, x)], axis=0)

    def col(x, lane_idx):
        return jnp.sum(jnp.where(lane_c == lane_idx, x, 0.0), axis=-1, keepdims=True)

    def gates(chunk, d):
        ab = ab_ref[pl.ds(pl.multiple_of(chunk * c, c), c), :]
        t = ab + dtb_ref[...]
        softplus = jnp.maximum(t, 0.0) + jnp.log(1.0 + jnp.exp(-jnp.abs(t)))
        g16 = -jnp.exp(alog_ref[...]) * softplus
        b16 = jax.nn.sigmoid(ab)
        h0 = d * GDN_HEADS + GDN_PAIR * pid
        bc = lambda x, i: jnp.broadcast_to(col(x, i), (c, LANE))
        g = jnp.concatenate([bc(g16, h0), bc(g16, h0 + 1)], axis=0)
        beta = jnp.concatenate([bc(b16, 2 * GDN_HEADS + h0), bc(b16, 2 * GDN_HEADS + h0 + 1)], axis=0)
        return g, beta

    def prep(cf, cb):
        rf = pl.multiple_of(cf * c, c)
        rb = pl.multiple_of(cb * c, c)
        k_f, k_b = stack(ks[pl.ds(rf, c), :]), stack(ks[pl.ds(rb, c), :])
        q_f, q_b = stack(qs[pl.ds(rf, c), :]), stack(qs[pl.ds(rb, c), :])
        k2 = _pair(k_f, k_b)
        q2 = _pair(q_f, q_b)
        v2 = _pair(stack(vs[pl.ds(rf, c), :]), stack(vs[pl.ds(rb, c), :]))
        g_f, beta_f = gates(cf, 0)
        g_b, beta_b = gates(cb, 1)
        beta2 = _pair(beta_f, beta_b)
        ghl = []
        for g in (g_f, g_b):
            hi = g.astype(BF16)
            ghl.append((hi, (g - hi.astype(F32)).astype(BF16)))
        wcum = jnp.concatenate(
            [jnp.concatenate([ghl[0][0], ghl[0][1], zero_bf, zero_bf], axis=1),
             jnp.concatenate([zero_bf, zero_bf, ghl[1][0], ghl[1][1]], axis=1)], axis=0)
        cum = jnp.dot(tri, wcum, preferred_element_type=F32)
        gram = _dot_nt(jnp.concatenate([k2, q2], axis=0), _block_diag(k2))
        yield
        gc_f = cum[:, 0:c2] + cum[:, c2:2 * c2]
        gc_b = cum[:, 2 * c2:3 * c2] + cum[:, 3 * c2:4 * c2]
        gc2 = _pair(gc_f, gc_b)
        dlog = _pair(gc_f - gc_f.T, gc_b - gc_b.T)
        decay = jnp.where(incl, jnp.exp(jnp.where(incl, dlog, 0.0)), 0.0)
        eg = jnp.exp(gc2)
        gl2 = _pair(jnp.where(top1, gc_f[c - 1:c, :], gc_f[c2 - 1:c2, :]),
                    jnp.where(top1, gc_b[0:1, :], gc_b[c:c + 1, :]))
        kk = gram[:c2]
        qk = gram[c2:]
        mm = jnp.where(strict, kk * decay * beta2, 0.0)
        attn = qk * decay
        tinv = eye - jnp.where((ri // 2) == (cc // 2), mm, 0.0)
        sz = 2
        while sz < c:
            nmask = ((ri // (2 * sz)) == (cc // (2 * sz))) & ((ri // sz) != (cc // sz))
            nt = _dot_pair(jnp.where(nmask, mm, 0.0), tinv)
            yield
            tinv = tinv - _dot_pair(tinv, nt)
            yield
            sz *= 2
        vb = (v2 * beta2).astype(BF16)
        kbe = (k2 * beta2 * eg).astype(BF16)
        rhs = jnp.concatenate(
            [jnp.concatenate([vb[:, :c2], kbe[:, :c2], zero_bf, zero_bf], axis=1),
             jnp.concatenate([zero_bf, zero_bf, vb[:, c2:], kbe[:, c2:]], axis=1)], axis=0)
        uw = jnp.dot(tinv.astype(BF16), rhs, preferred_element_type=F32).astype(BF16)
        yield
        kg = k2 * jnp.exp(gl2 - gc2)
        lhs = jnp.concatenate([_pair(kg[:, :c2].T, kg[:, c2:].T), attn], axis=0).astype(BF16)
        zz = jnp.zeros((c2, 2 * c2), BF16)
        res = jnp.dot(lhs, jnp.concatenate([jnp.concatenate([uw[:, :2 * c2], zz], axis=1),
                                            jnp.concatenate([zz, uw[:, 2 * c2:]], axis=1)], axis=0),
                      preferred_element_type=F32)
        yield
        b2 = _pair(res[:c2, 0:c2], res[:c2, 2 * c2:3 * c2])
        wk = _pair(res[:c2, c2:2 * c2], res[:c2, 3 * c2:4 * c2])
        o2 = _pair(res[c2:, 0:c2], res[c2:, 2 * c2:3 * c2])
        aw = _pair(res[c2:, c2:2 * c2], res[c2:, 3 * c2:4 * c2])
        qp = q2 * eg - aw
        return jnp.concatenate([wk, qp], axis=0).astype(BF16), b2, o2, jnp.exp(gl2)

    def scan(base, state, pres):
        for j, (lhs, b2, o2, egl) in enumerate(pres):
            cf = base + j
            cb = n - 1 - cf
            r = jnp.dot(lhs, _block_diag(state.astype(BF16)), preferred_element_type=F32)
            out = r[c2:] + o2
            state = state * egl + b2 - r[:c2]
            of_s[pl.ds(pl.multiple_of(cf * c, c), c), :] = out[:c, :c2] + out[c:, :c2]
            ob_s[pl.ds(pl.multiple_of(cb * c, c), c), :] = out[:c, c2:] + out[c:, c2:]
            yield
        return state

    def run_lockstep(gens):
        results = [None] * len(gens)
        live = list(range(len(gens)))
        while live:
            for i in list(live):
                try:
                    next(gens[i])
                except StopIteration as stop:
                    results[i] = stop.value
                    live.remove(i)
        return results

    n_groups = n // unroll

    def preps(group):
        base = group * unroll
        return [prep(base + j, n - 1 - (base + j)) for j in range(unroll)]

    def body(g, carry):
        state, pres = carry
        out = run_lockstep(preps(g) + [scan((g - 1) * unroll, state, pres)])
        return out[-1], out[:-1]

    pres0 = run_lockstep(preps(0) + [rows_gen(prepare_rows, mid_blocks)])[:unroll]
    state, pres = lax.fori_loop(1, n_groups, body, (jnp.zeros((c2, 2 * c2), F32), pres0))
    run_lockstep([scan((n_groups - 1) * unroll, state, pres), rows_gen(finish_rows, mid_blocks)])
    for r0 in edge_blocks:
        finish_rows(r0, group_rows)


def _gdn(qkv, z, lat, conv_w, alog_l, dtb_l, gn_l, b, s):
    npair = GDN_HEADS // GDN_PAIR
    nqk = GDN_HEADS * GDN_DK // LANE
    n_chunks = s // GDN_CHUNK
    unroll = math.gcd(n_chunks, GDN_UNROLL)
    seq_blk = lambda off: pl.BlockSpec((s, LANE), lambda bi, pi: (bi, off + pi))
    conv_blk = lambda off: pl.BlockSpec((CONV_WIDTH, LANE), lambda bi, pi: (0, off + pi))
    vec_blk = pl.BlockSpec((1, LANE), lambda bi, pi: (0, 0))
    return pl.pallas_call(
        functools.partial(_gdn_kernel, unroll=unroll),
        grid=(b, npair),
        in_specs=[seq_blk(0), seq_blk(nqk), seq_blk(2 * nqk),
                  pl.BlockSpec((s, LANE), lambda bi, pi: (bi, pi)),
                  pl.BlockSpec((s, LANE), lambda bi, pi: (bi, LAT_AB // LANE)),
                  conv_blk(0), conv_blk(nqk), conv_blk(2 * nqk),
                  vec_blk, vec_blk, vec_blk],
        out_specs=pl.BlockSpec((s, LANE), lambda bi, pi: (bi, pi)),
        out_shape=jax.ShapeDtypeStruct((b * s, GDN_HEADS * GDN_DV), F32),
        scratch_shapes=[pltpu.VMEM((s, LANE), F32)] * 5,
        compiler_params=_cparams(2),
        name="gdn_mixer",
    )(qkv, qkv, qkv, z, lat, conv_w, conv_w, conv_w, alog_l, dtb_l, gn_l)


def _out_ffn_kernel(x_ref, ym_ref, yg_ref, wo_ref, nf_ref, wg_ref, wu_ref, wd_ref, nfin_ref,
                    o_ref, *, final, ff_chunk):
    y = jnp.concatenate([ym_ref[...].astype(BF16), yg_ref[...].astype(BF16)], axis=1)
    x = x_ref[...] + jnp.dot(y, wo_ref[...], preferred_element_type=F32)
    hn = _rms(x, nf_ref[...]).astype(BF16)
    o_ref[...] = x
    for lo in range(0, D_FF, ff_chunk):
        sl = slice(lo, min(lo + ff_chunk, D_FF))
        g = jnp.dot(hn, wg_ref[:, sl], preferred_element_type=F32)
        u = jnp.dot(hn, wu_ref[:, sl], preferred_element_type=F32)
        o_ref[...] += jnp.dot((_silu(g) * u).astype(BF16), wd_ref[sl, :], preferred_element_type=F32)
    if final:
        o_ref[...] = _rms(o_ref[...], nfin_ref[...])


def _out_ffn(x, ym, yg, wo, nf, wg, wu, wd, nfin, tm, final):
    t = x.shape[0]
    const = lambda shape: pl.BlockSpec(shape, lambda i: (0, 0), pipeline_mode=pl.Buffered(1))
    return pl.pallas_call(
        functools.partial(_out_ffn_kernel, final=final, ff_chunk=FF_CHUNK),
        grid=(t // tm,),
        in_specs=[pl.BlockSpec((tm, D_MODEL), lambda i: (i, 0)),
                  pl.BlockSpec((tm, ym.shape[1]), lambda i: (i, 0)),
                  pl.BlockSpec((tm, yg.shape[1]), lambda i: (i, 0)),
                  const(wo.shape), const((1, D_MODEL)), const(wg.shape), const(wu.shape),
                  const(wd.shape), const((1, D_MODEL))],
        out_specs=pl.BlockSpec((tm, D_MODEL), lambda i: (i, 0)),
        out_shape=jax.ShapeDtypeStruct((t, D_MODEL), F32),
        compiler_params=_cparams(1),
        name="out_ffn",
    )(x, ym, yg, wo, nf, wg, wu, wd, nfin)


def _swap_halves(w):
    r = w.shape[-1] // 2
    return jnp.concatenate([w[..., r:], w[..., :r]], axis=-1)


def _prep_layer(w_in, w_q_up, w_kv_up):
    d = w_in.shape[0]
    o_kv = Q_LORA
    o_kr = o_kv + KV_LORA
    o_qkv = o_kr + MLA_ROPE
    o_z = o_qkv + QKV_W
    o_a = o_z + Z_W
    kr = w_in[:, o_kr:o_qkv]
    pad_rope = lambda w: jnp.concatenate(
        [jnp.zeros((d, MLA_NOPE), F32), w, jnp.zeros((d, HEAD_BLOCK - MLA_NOPE - MLA_ROPE), F32)], axis=1)
    ab = jnp.concatenate([w_in[:, o_a:], jnp.zeros((d, LANE - 4 * GDN_HEADS), F32)], axis=1)
    w_in_p = jnp.concatenate([w_in[:, :o_kr], pad_rope(kr), pad_rope(_swap_halves(kr)), ab,
                              w_in[:, o_qkv:o_z], w_in[:, o_z:o_a]], axis=1).astype(BF16)

    dq = MLA_NOPE + MLA_ROPE
    wq = w_q_up.reshape(Q_LORA, MLA_HEADS, dq)
    zpad = jnp.zeros((Q_LORA, MLA_HEADS, HEAD_BLOCK - dq), F32)
    wq_a = jnp.concatenate([wq, zpad], axis=-1)
    wq_b = jnp.concatenate([jnp.zeros((Q_LORA, MLA_HEADS, MLA_NOPE), F32),
                            _swap_halves(wq[..., MLA_NOPE:]), zpad], axis=-1)
    wq_p = jnp.concatenate([wq_a.reshape(Q_LORA, -1), wq_b.reshape(Q_LORA, -1)], axis=1).astype(BF16)

    wkv = w_kv_up.reshape(KV_LORA, MLA_HEADS, MLA_NOPE + MLA_V)
    wk = jnp.concatenate([wkv[..., :MLA_NOPE],
                          jnp.zeros((KV_LORA, MLA_HEADS, HEAD_BLOCK - MLA_NOPE), F32)], axis=-1)
    wk_p = wk.reshape(KV_LORA, -1).astype(BF16)
    wvt_p = wkv[..., MLA_NOPE:].reshape(KV_LORA, -1).T.astype(BF16)
    return w_in_p, wq_p, wk_p, wvt_p


def _rope_table(s):
    pos = jnp.arange(s, dtype=F32)
    inv = ROPE_THETA ** (-jnp.arange(0, MLA_ROPE, 2, dtype=F32) / MLA_ROPE)
    ang = pos[:, None] * inv[None, :]
    c, sn = jnp.cos(ang), jnp.sin(ang)
    cc = jnp.concatenate([c, c], axis=1)
    ss = jnp.concatenate([-sn, sn], axis=1)
    z_lo = jnp.zeros((s, MLA_NOPE), F32)
    z_hi = jnp.zeros((s, HEAD_BLOCK - MLA_NOPE - MLA_ROPE), F32)
    scale = (MLA_NOPE + MLA_ROPE) ** -0.5 * math.log2(math.e)
    ccq = jnp.concatenate([jnp.ones((s, MLA_NOPE), F32), cc, z_hi], axis=1) * scale
    ssq = jnp.concatenate([z_lo, ss, z_hi], axis=1) * scale
    cck = jnp.concatenate([z_lo, cc, z_hi], axis=1)
    ssk = jnp.concatenate([z_lo, ss, z_hi], axis=1)
    return jnp.concatenate([ccq, ssq, cck, ssk], axis=1)


def _lane_vec(v):
    v = v.reshape(1, -1).astype(F32)
    return jnp.concatenate([v, jnp.zeros((1, LANE - v.shape[1]), F32)], axis=1)


def _pick_tile(n, pref):
    t = min(pref, n)
    while n % t:
        t //= 2
    return t


def _trunk(x, layers, norm_final, tab):
    b, s, d = x.shape
    t = b * s
    tm = _pick_tile(s, 512)
    tq = _pick_tile(s, 512)
    kb = _pick_tile(s, 1024)
    h = x.reshape(t, d)
    depth = len(layers)
    for li, lw in enumerate(layers):
        lat, qkv, z = _in_proj(h, lw["norm_attn"], lw["w_in"], tm)
        q, k, v = _mla_proj(lat, tab, lw["q_norm"], lw["kv_norm"], lw["wq"], lw["wk"], lw["wvt"], b, s, tm)
        y_mla = _attention(q, k, v, tq, kb)
        y_gdn = _gdn(qkv, z, lat, lw["conv_w"], lw["a_log"], lw["dt_bias"], lw["gdn_norm"], b, s)
        h = _out_ffn(h, y_mla, y_gdn, lw["w_out"], lw["norm_ffn"], lw["w_gate"], lw["w_up"],
                     lw["w_down"], norm_final, tm, final=(li == depth - 1))
    return h.reshape(b, s, d)


def kernel(x_prompt, x_sample, norm_attn, w_in, q_norm, w_q_up, kv_norm, w_kv_up, conv_w, a_log, dt_bias, gdn_norm, w_out, norm_ffn, w_gate, w_up, w_down, norm_final):
    depth = w_in.shape[0]
    layers = []
    for l in range(depth):
        w_in_p, wq_p, wk_p, wvt_p = _prep_layer(w_in[l], w_q_up[l], w_kv_up[l])
        layers.append(dict(
            norm_attn=norm_attn[l].reshape(1, -1), w_in=w_in_p,
            q_norm=q_norm[l].reshape(1, -1), kv_norm=kv_norm[l].reshape(1, -1),
            wq=wq_p, wk=wk_p, wvt=wvt_p, conv_w=conv_w[l],
            a_log=_lane_vec(a_log[l]), dt_bias=_lane_vec(dt_bias[l]),
            gdn_norm=jnp.tile(gdn_norm[l].reshape(1, -1), (1, GDN_PAIR)),
            w_out=w_out[l].astype(BF16), norm_ffn=norm_ffn[l].reshape(1, -1),
            w_gate=w_gate[l].astype(BF16), w_up=w_up[l].astype(BF16),
            w_down=w_down[l].astype(BF16)))
    nfin = norm_final.reshape(1, -1)
    assert x_prompt.shape[1] == x_sample.shape[1]
    tab = _rope_table(x_prompt.shape[1])
    y_prompt = _trunk(x_prompt, layers, nfin, tab)
    y_sample = _trunk(x_sample, layers, nfin, tab)
    return (y_prompt, y_sample)
```

```python
import functools
import math

import jax
import jax.numpy as jnp
from jax import lax
from jax.experimental import pallas as pl
from jax.experimental.pallas import tpu as pltpu

F32 = jnp.float32
BF16 = jnp.bfloat16

D_MODEL = 1024
MLA_HEADS = 8
MLA_NOPE = 64
MLA_ROPE = 32
MLA_V = 64
Q_LORA = 256
KV_LORA = 128
ROPE_THETA = 10000.0
GDN_HEADS = 8
GDN_DK = 64
GDN_DV = 64
CONV_WIDTH = 3
D_FF = 2816
NORM_EPS = 1e-6

LANE = 128
HEAD_BLOCK = 128
GDN_CHUNK = 64
GDN_PAIR = 2
VMEM_LIMIT = 56 * 1024 * 1024
FF_CHUNK = 1536
GDN_UNROLL = 8

LAT_Q = 0
LAT_KV = Q_LORA
LAT_KR = LAT_KV + KV_LORA
LAT_KRS = LAT_KR + LANE
LAT_AB = LAT_KRS + LANE
LAT_W = LAT_AB + LANE
QKV_W = GDN_HEADS * (2 * GDN_DK + GDN_DV)
Z_W = GDN_HEADS * GDN_DV
IN_W = LAT_W + QKV_W + Z_W


def _dot(a, b):
    return jnp.dot(a.astype(BF16), b.astype(BF16), preferred_element_type=F32)


def _dot_nt(a, b):
    return lax.dot_general(a.astype(BF16), b.astype(BF16), (((1,), (1,)), ((), ())),
                           preferred_element_type=F32)


def _rms(x, g):
    return x * lax.rsqrt(jnp.mean(x * x, axis=-1, keepdims=True) + NORM_EPS) * g


def _silu(x):
    return x * jax.nn.sigmoid(x)


def _cparams(n_axes):
    return pltpu.CompilerParams(dimension_semantics=("arbitrary",) * n_axes,
                                vmem_limit_bytes=VMEM_LIMIT)


def _in_proj_kernel(x_ref, g_ref, w_ref, lat_ref, qkv_ref, z_ref):
    xn = _rms(x_ref[...], g_ref[...]).astype(BF16)
    lat_ref[...] = jnp.dot(xn, w_ref[:, :LAT_W], preferred_element_type=F32)
    qkv_ref[...] = jnp.dot(xn, w_ref[:, LAT_W:LAT_W + QKV_W], preferred_element_type=F32)
    z_ref[...] = jnp.dot(xn, w_ref[:, LAT_W + QKV_W:], preferred_element_type=F32)


def _in_proj(x, g, w, tm):
    t = x.shape[0]
    return pl.pallas_call(
        _in_proj_kernel,
        grid=(t // tm,),
        in_specs=[pl.BlockSpec((tm, D_MODEL), lambda i: (i, 0)),
                  pl.BlockSpec((1, D_MODEL), lambda i: (0, 0)),
                  pl.BlockSpec((D_MODEL, IN_W), lambda i: (0, 0))],
        out_specs=[pl.BlockSpec((tm, LAT_W), lambda i: (i, 0)),
                   pl.BlockSpec((tm, QKV_W), lambda i: (i, 0)),
                   pl.BlockSpec((tm, Z_W), lambda i: (i, 0))],
        out_shape=[jax.ShapeDtypeStruct((t, LAT_W), F32),
                   jax.ShapeDtypeStruct((t, QKV_W), F32),
                   jax.ShapeDtypeStruct((t, Z_W), F32)],
        compiler_params=_cparams(1),
        name="in_proj",
    )(x, g, w)


def _mla_proj_kernel(lat_ref, tab_ref, qn_ref, kvn_ref, wq_ref, wk_ref, wvt_ref, q_ref, k_ref, vt_ref):
    hw = MLA_HEADS * HEAD_BLOCK
    qn = _rms(lat_ref[:, LAT_Q:LAT_Q + Q_LORA], qn_ref[...]).astype(BF16)
    qa = jnp.dot(qn, wq_ref[:, :hw], preferred_element_type=F32)
    qb = jnp.dot(qn, wq_ref[:, hw:], preferred_element_type=F32)
    ccq = tab_ref[:, 0:LANE]
    ssq = tab_ref[:, LANE:2 * LANE]
    cck = tab_ref[:, 2 * LANE:3 * LANE]
    ssk = tab_ref[:, 3 * LANE:4 * LANE]
    kvn = _rms(lat_ref[:, LAT_KV:LAT_KV + KV_LORA], kvn_ref[...]).astype(BF16)
    ka = jnp.dot(kvn, wk_ref[...], preferred_element_type=F32)
    vt = lax.dot_general(wvt_ref[...], kvn, (((1,), (1,)), ((), ())), preferred_element_type=F32)
    kpe = lat_ref[:, LAT_KR:LAT_KR + LANE] * cck + lat_ref[:, LAT_KRS:LAT_KRS + LANE] * ssk
    for h in range(MLA_HEADS):
        sl = slice(h * HEAD_BLOCK, (h + 1) * HEAD_BLOCK)
        q_ref[0, h] = (qa[:, sl] * ccq + qb[:, sl] * ssq).astype(BF16)
        k_ref[0, h] = (ka[:, sl] + kpe).astype(BF16)
    for p in range(MLA_HEADS // 2):
        vt_ref[0, p] = vt[p * LANE:(p + 1) * LANE, :].astype(BF16)


def _mla_proj(lat, tab, qn, kvn, wq, wk, wvt, b, s, tm):
    nt = s // tm
    hw = MLA_HEADS * HEAD_BLOCK
    return pl.pallas_call(
        _mla_proj_kernel,
        grid=(b, nt),
        in_specs=[pl.BlockSpec((tm, LAT_AB), lambda bi, si: (bi * nt + si, 0)),
                  pl.BlockSpec((tm, 4 * LANE), lambda bi, si: (si, 0)),
                  pl.BlockSpec((1, Q_LORA), lambda bi, si: (0, 0)),
                  pl.BlockSpec((1, KV_LORA), lambda bi, si: (0, 0)),
                  pl.BlockSpec((Q_LORA, 2 * hw), lambda bi, si: (0, 0)),
                  pl.BlockSpec((KV_LORA, hw), lambda bi, si: (0, 0)),
                  pl.BlockSpec((MLA_HEADS * MLA_V, KV_LORA), lambda bi, si: (0, 0))],
        out_specs=[pl.BlockSpec((1, MLA_HEADS, tm, HEAD_BLOCK), lambda bi, si: (bi, 0, si, 0)),
                   pl.BlockSpec((1, MLA_HEADS, tm, HEAD_BLOCK), lambda bi, si: (bi, 0, si, 0)),
                   pl.BlockSpec((1, MLA_HEADS // 2, LANE, tm), lambda bi, si: (bi, 0, 0, si))],
        out_shape=[jax.ShapeDtypeStruct((b, MLA_HEADS, s, HEAD_BLOCK), BF16),
                   jax.ShapeDtypeStruct((b, MLA_HEADS, s, HEAD_BLOCK), BF16),
                   jax.ShapeDtypeStruct((b, MLA_HEADS // 2, LANE, s), BF16)],
        compiler_params=_cparams(2),
        name="mla_proj",
    )(lat, tab, qn, kvn, wq, wk, wvt)


def _attn_kernel(q_ref, k_ref, vt_ref, o_ref, *, kb):
    s = k_ref.shape[2]
    tq = q_ref.shape[2]
    units = [(h, j) for h in range(2) for j in range(s // kb)]

    def scores(unit):
        h, j = unit
        return lax.dot_general(k_ref[0, h, j * kb:(j + 1) * kb, :], q_ref[0, h],
                               (((1,), (1,)), ((), ())), preferred_element_type=F32)

    stats = [(jnp.full((1, tq), -jnp.inf, F32), jnp.zeros((1, tq), F32), jnp.zeros((MLA_V, tq), F32))] * 2
    sc_next = scores(units[0])
    for idx, (h, j) in enumerate(units):
        sc = sc_next
        if idx + 1 < len(units):
            sc_next = scores(units[idx + 1])
        m, l, acc = stats[h]
        m_new = jnp.maximum(m, jnp.max(sc, axis=0, keepdims=True))
        alpha = jnp.exp2(m - m_new)
        p = jnp.exp2(sc - m_new)
        l = alpha * l + jnp.sum(p, axis=0, keepdims=True)
        vt = vt_ref[0, 0, h * MLA_V:(h + 1) * MLA_V, j * kb:(j + 1) * kb]
        acc = alpha * acc + jnp.dot(vt, p.astype(BF16), preferred_element_type=F32)
        stats[h] = (m_new, l, acc)
    o_ref[...] = jnp.concatenate([acc / l for _, l, acc in stats], axis=0).T


def _attention(q, k, v, tq, kb):
    b, h, s, _ = q.shape
    nq = s // tq
    return pl.pallas_call(
        functools.partial(_attn_kernel, kb=kb),
        grid=(b, h // 2, nq),
        in_specs=[pl.BlockSpec((1, 2, tq, HEAD_BLOCK), lambda bi, pi, qi: (bi, pi, qi, 0)),
                  pl.BlockSpec((1, 2, s, HEAD_BLOCK), lambda bi, pi, qi: (bi, pi, 0, 0)),
                  pl.BlockSpec((1, 1, LANE, s), lambda bi, pi, qi: (bi, pi, 0, 0))],
        out_specs=pl.BlockSpec((tq, LANE), lambda bi, pi, qi: (bi * nq + qi, pi)),
        out_shape=jax.ShapeDtypeStruct((b * s, h * MLA_V), F32),
        compiler_params=_cparams(3),
        name="mla_attention",
    )(q, k, v)


def _half_sums(x, lane_lo):
    s0 = jnp.sum(jnp.where(lane_lo, x, 0.0), axis=-1, keepdims=True)
    s1 = jnp.sum(jnp.where(lane_lo, 0.0, x), axis=-1, keepdims=True)
    return jnp.where(lane_lo, s0, s1)


def _pair(a, b):
    return jnp.concatenate([a, b], axis=1)


def _block_diag(x2):
    w = x2.shape[1] // 2
    z = jnp.zeros((x2.shape[0], w), x2.dtype)
    return jnp.concatenate([jnp.concatenate([x2[:, :w], z], axis=1),
                            jnp.concatenate([z, x2[:, w:]], axis=1)], axis=0)


def _dot_pair(a2, b2):
    return jnp.dot(a2.astype(BF16), _block_diag(b2.astype(BF16)), preferred_element_type=F32)


def _gdn_kernel(q_ref, k_ref, v_ref, ab_ref, cq_ref, ck_ref, cv_ref, alog_ref, dtb_ref,
                y_ref, qs, ks, vs, of_s, ob_s, *, unroll):
    s = q_ref.shape[0]
    c = GDN_CHUNK
    n = s // c
    c2 = 2 * c
    pid = pl.program_id(1)

    def conv_silu(x_ref, w_ref, r0, rows):
        zrow = jnp.zeros((1, LANE), F32)
        x = x_ref[r0:r0 + rows, :]
        prev = (jnp.concatenate([zrow, x_ref[0:rows - 1, :]], axis=0) if r0 == 0
                else x_ref[r0 - 1:r0 - 1 + rows, :])
        nxt = (jnp.concatenate([x_ref[r0 + 1:s, :], zrow], axis=0) if r0 + rows == s
               else x_ref[r0 + 1:r0 + 1 + rows, :])
        return _silu(prev * w_ref[0:1, :] + x * w_ref[1:2, :] + nxt * w_ref[2:3, :])

    def l2n(x):
        lo = lax.broadcasted_iota(jnp.int32, x.shape, 1) < GDN_DK
        return x * lax.rsqrt(_half_sums(x * x, lo) + NORM_EPS)

    def prepare_rows(r0, rows):
        qs[r0:r0 + rows, :] = l2n(conv_silu(q_ref, cq_ref, r0, rows)) * (GDN_DK ** -0.5)
        ks[r0:r0 + rows, :] = l2n(conv_silu(k_ref, ck_ref, r0, rows))
        vs[r0:r0 + rows, :] = conv_silu(v_ref, cv_ref, r0, rows)

    def finish_rows(r0, rows):
        y_ref[r0:r0 + rows, :] = of_s[r0:r0 + rows, :] + ob_s[r0:r0 + rows, :]

    def rows_gen(fn, blocks):
        for r0 in blocks:
            fn(r0, group_rows)
            yield

    group_rows = unroll * c
    edge_blocks = sorted({0, s - group_rows})
    mid_blocks = [r for r in range(0, s, group_rows) if r not in edge_blocks]
    for r0 in edge_blocks:
        prepare_rows(r0, group_rows)

    ri = lax.broadcasted_iota(jnp.int32, (c2, 2 * c2), 0)
    cj = lax.broadcasted_iota(jnp.int32, (c2, 2 * c2), 1)
    is_b = cj >= c2
    cc = jnp.where(is_b, cj - c2, cj)
    same_head = (ri // c) == (cc // c)
    incl = same_head & ((is_b & (ri <= cc)) | (jnp.logical_not(is_b) & (ri >= cc)))
    strict = incl & (ri != cc)
    eye = (ri == cc).astype(F32)
    tri = incl.astype(BF16)
    top = ri < c
    top1 = top[:, :c2]
    lane_c = lax.broadcasted_iota(jnp.int32, (c, LANE), 1)
    lo_c = lane_c < GDN_DK
    zero_bf = jnp.zeros((c2, c2), BF16)

    def stack(x):
        return jnp.concatenate([jnp.where(lo_c, x, 0.0), jnp.where(lo_c, 0.0, x)], axis=0)

    def col(x, lane_idx):
        return jnp.sum(jnp.where(lane_c == lane_idx, x, 0.0), axis=-1, keepdims=True)

    def gates(chunk, d):
        ab = ab_ref[pl.ds(pl.multiple_of(chunk * c, c), c), :]
        t = ab + dtb_ref[...]
        softplus = jnp.maximum(t, 0.0) + jnp.log(1.0 + jnp.exp(-jnp.abs(t)))
        g16 = -jnp.exp(alog_ref[...]) * softplus
        b16 = jax.nn.sigmoid(ab)
        h0 = d * GDN_HEADS + GDN_PAIR * pid
        bc = lambda x, i: jnp.broadcast_to(col(x, i), (c, LANE))
        g = jnp.concatenate([bc(g16, h0), bc(g16, h0 + 1)], axis=0)
        beta = jnp.concatenate([bc(b16, 2 * GDN_HEADS + h0), bc(b16, 2 * GDN_HEADS + h0 + 1)], axis=0)
        return g, beta

    def prep(cf, cb):
        rf = pl.multiple_of(cf * c, c)
        rb = pl.multiple_of(cb * c, c)
        k_f, k_b = stack(ks[pl.ds(rf, c), :]), stack(ks[pl.ds(rb, c), :])
        q_f, q_b = stack(qs[pl.ds(rf, c), :]), stack(qs[pl.ds(rb, c), :])
        k2 = _pair(k_f, k_b)
        q2 = _pair(q_f, q_b)
        v2 = _pair(stack(vs[pl.ds(rf, c), :]), stack(vs[pl.ds(rb, c), :]))
        g_f, beta_f = gates(cf, 0)
        g_b, beta_b = gates(cb, 1)
        beta2 = _pair(beta_f, beta_b)
        ghl = []
        for g in (g_f, g_b):
            hi = g.astype(BF16)
            ghl.append((hi, (g - hi.astype(F32)).astype(BF16)))
        wcum = jnp.concatenate(
            [jnp.concatenate([ghl[0][0], ghl[0][1], zero_bf, zero_bf], axis=1),
             jnp.concatenate([zero_bf, zero_bf, ghl[1][0], ghl[1][1]], axis=1)], axis=0)
        cum = jnp.dot(tri, wcum, preferred_element_type=F32)
        gram = _dot_nt(jnp.concatenate([k2, q2], axis=0), _block_diag(k2))
        yield
        gc_f = cum[:, 0:c2] + cum[:, c2:2 * c2]
        gc_b = cum[:, 2 * c2:3 * c2] + cum[:, 3 * c2:4 * c2]
        gc2 = _pair(gc_f, gc_b)
        dlog = _pair(gc_f - gc_f.T, gc_b - gc_b.T)
        decay = jnp.where(incl, jnp.exp(jnp.where(incl, dlog, 0.0)), 0.0)
        eg = jnp.exp(gc2)
        gl2 = _pair(jnp.where(top1, gc_f[c - 1:c, :], gc_f[c2 - 1:c2, :]),
                    jnp.where(top1, gc_b[0:1, :], gc_b[c:c + 1, :]))
        kk = gram[:c2]
        qk = gram[c2:]
        mm = jnp.where(strict, kk * decay * beta2, 0.0)
        attn = qk * decay
        tinv = eye - jnp.where((ri // 2) == (cc // 2), mm, 0.0)
        sz = 2
        while sz < c:
            nmask = ((ri // (2 * sz)) == (cc // (2 * sz))) & ((ri // sz) != (cc // sz))
            nt = _dot_pair(jnp.where(nmask, mm, 0.0), tinv)
            yield
            tinv = tinv - _dot_pair(tinv, nt)
            yield
            sz *= 2
        vb = (v2 * beta2).astype(BF16)
        kbe = (k2 * beta2 * eg).astype(BF16)
        rhs = jnp.concatenate(
            [jnp.concatenate([vb[:, :c2], kbe[:, :c2], zero_bf, zero_bf], axis=1),
             jnp.concatenate([zero_bf, zero_bf, vb[:, c2:], kbe[:, c2:]], axis=1)], axis=0)
        uw = jnp.dot(tinv.astype(BF16), rhs, preferred_element_type=F32).astype(BF16)
        yield
        kg = k2 * jnp.exp(gl2 - gc2)
        lhs = jnp.concatenate([_pair(kg[:, :c2].T, kg[:, c2:].T), attn], axis=0).astype(BF16)
        zz = jnp.zeros((c2, 2 * c2), BF16)
        res = jnp.dot(lhs, jnp.concatenate([jnp.concatenate([uw[:, :2 * c2], zz], axis=1),
                                            jnp.concatenate([zz, uw[:, 2 * c2:]], axis=1)], axis=0),
                      preferred_element_type=F32)
        yield
        b2 = _pair(res[:c2, 0:c2], res[:c2, 2 * c2:3 * c2])
        wk = _pair(res[:c2, c2:2 * c2], res[:c2, 3 * c2:4 * c2])
        o2 = _pair(res[c2:, 0:c2], res[c2:, 2 * c2:3 * c2])
        aw = _pair(res[c2:, c2:2 * c2], res[c2:, 3 * c2:4 * c2])
        qp = q2 * eg - aw
        return jnp.concatenate([wk, qp], axis=0).astype(BF16), b2, o2, jnp.exp(gl2)

    def scan(base, state, pres):
        for j, (lhs, b2, o2, egl) in enumerate(pres):
            cf = base + j
            cb = n - 1 - cf
            r = jnp.dot(lhs, _block_diag(state.astype(BF16)), preferred_element_type=F32)
            out = r[c2:] + o2
            state = state * egl + b2 - r[:c2]
            of_s[pl.ds(pl.multiple_of(cf * c, c), c), :] = out[:c, :c2] + out[c:, :c2]
            ob_s[pl.ds(pl.multiple_of(cb * c, c), c), :] = out[:c, c2:] + out[c:, c2:]
            yield
        return state

    def run_lockstep(gens):
        results = [None] * len(gens)
        live = list(range(len(gens)))
        while live:
            for i in list(live):
                try:
                    next(gens[i])
                except StopIteration as stop:
                    results[i] = stop.value
                    live.remove(i)
        return results

    n_groups = n // unroll

    def preps(group):
        base = group * unroll
        return [prep(base + j, n - 1 - (base + j)) for j in range(unroll)]

    def body(g, carry):
        state, pres = carry
        out = run_lockstep(preps(g) + [scan((g - 1) * unroll, state, pres)])
        return out[-1], out[:-1]

    pres0 = run_lockstep(preps(0) + [rows_gen(prepare_rows, mid_blocks)])[:unroll]
    state, pres = lax.fori_loop(1, n_groups, body, (jnp.zeros((c2, 2 * c2), F32), pres0))
    run_lockstep([scan((n_groups - 1) * unroll, state, pres), rows_gen(finish_rows, mid_blocks)])
    for r0 in edge_blocks:
        finish_rows(r0, group_rows)


def _gdn(qkv, lat, conv_w, alog_l, dtb_l, b, s):
    npair = GDN_HEADS // GDN_PAIR
    nqk = GDN_HEADS * GDN_DK // LANE
    n_chunks = s // GDN_CHUNK
    unroll = math.gcd(n_chunks, GDN_UNROLL)
    seq_blk = lambda off: pl.BlockSpec((s, LANE), lambda bi, pi: (bi, off + pi))
    conv_blk = lambda off: pl.BlockSpec((CONV_WIDTH, LANE), lambda bi, pi: (0, off + pi))
    vec_blk = pl.BlockSpec((1, LANE), lambda bi, pi: (0, 0))
    return pl.pallas_call(
        functools.partial(_gdn_kernel, unroll=unroll),
        grid=(b, npair),
        in_specs=[seq_blk(0), seq_blk(nqk), seq_blk(2 * nqk),
                  pl.BlockSpec((s, LANE), lambda bi, pi: (bi, LAT_AB // LANE)),
                  conv_blk(0), conv_blk(nqk), conv_blk(2 * nqk),
                  vec_blk, vec_blk],
        out_specs=pl.BlockSpec((s, LANE), lambda bi, pi: (bi, pi)),
        out_shape=jax.ShapeDtypeStruct((b * s, GDN_HEADS * GDN_DV), F32),
        scratch_shapes=[pltpu.VMEM((s, LANE), F32)] * 5,
        compiler_params=_cparams(2),
        name="gdn_mixer",
    )(qkv, qkv, qkv, lat, conv_w, conv_w, conv_w, alog_l, dtb_l)


def _out_ffn_kernel(x_ref, ym_ref, og_ref, z_ref, gn_ref, wo_ref, nf_ref, wg_ref, wu_ref, wd_ref,
                    nfin_ref, o_ref, *, final, ff_chunk):
    lo = lax.broadcasted_iota(jnp.int32, (og_ref.shape[0], LANE), 1) < GDN_DV
    yg = []
    for p in range(og_ref.shape[1] // LANE):
        o = og_ref[:, p * LANE:(p + 1) * LANE]
        ms = _half_sums(o * o, lo) * (1.0 / GDN_DV)
        yg.append((o * lax.rsqrt(ms + NORM_EPS) * gn_ref[...]
                   * _silu(z_ref[:, p * LANE:(p + 1) * LANE])).astype(BF16))
    y = jnp.concatenate([ym_ref[...].astype(BF16)] + yg, axis=1)
    x = x_ref[...] + jnp.dot(y, wo_ref[...], preferred_element_type=F32)
    hn = _rms(x, nf_ref[...]).astype(BF16)
    o_ref[...] = x
    for lo in range(0, D_FF, ff_chunk):
        sl = slice(lo, min(lo + ff_chunk, D_FF))
        g = jnp.dot(hn, wg_ref[:, sl], preferred_element_type=F32)
        u = jnp.dot(hn, wu_ref[:, sl], preferred_element_type=F32)
        o_ref[...] += jnp.dot((_silu(g) * u).astype(BF16), wd_ref[sl, :], preferred_element_type=F32)
    if final:
        o_ref[...] = _rms(o_ref[...], nfin_ref[...])


def _out_ffn(x, ym, og, z, gn, wo, nf, wg, wu, wd, nfin, tm, final):
    t = x.shape[0]
    const = lambda shape: pl.BlockSpec(shape, lambda i: (0, 0), pipeline_mode=pl.Buffered(1))
    return pl.pallas_call(
        functools.partial(_out_ffn_kernel, final=final, ff_chunk=FF_CHUNK),
        grid=(t // tm,),
        in_specs=[pl.BlockSpec((tm, D_MODEL), lambda i: (i, 0)),
                  pl.BlockSpec((tm, ym.shape[1]), lambda i: (i, 0)),
                  pl.BlockSpec((tm, og.shape[1]), lambda i: (i, 0)),
                  pl.BlockSpec((tm, z.shape[1]), lambda i: (i, 0)),
                  const((1, LANE)),
                  const(wo.shape), const((1, D_MODEL)), const(wg.shape), const(wu.shape),
                  const(wd.shape), const((1, D_MODEL))],
        out_specs=pl.BlockSpec((tm, D_MODEL), lambda i: (i, 0)),
        out_shape=jax.ShapeDtypeStruct((t, D_MODEL), F32),
        compiler_params=_cparams(1),
        name="out_ffn",
    )(x, ym, og, z, gn, wo, nf, wg, wu, wd, nfin)


def _swap_halves(w):
    r = w.shape[-1] // 2
    return jnp.concatenate([w[..., r:], w[..., :r]], axis=-1)


def _prep_layer(w_in, w_q_up, w_kv_up):
    d = w_in.shape[0]
    o_kv = Q_LORA
    o_kr = o_kv + KV_LORA
    o_qkv = o_kr + MLA_ROPE
    o_z = o_qkv + QKV_W
    o_a = o_z + Z_W
    kr = w_in[:, o_kr:o_qkv]
    pad_rope = lambda w: jnp.concatenate(
        [jnp.zeros((d, MLA_NOPE), F32), w, jnp.zeros((d, HEAD_BLOCK - MLA_NOPE - MLA_ROPE), F32)], axis=1)
    ab = jnp.concatenate([w_in[:, o_a:], jnp.zeros((d, LANE - 4 * GDN_HEADS), F32)], axis=1)
    w_in_p = jnp.concatenate([w_in[:, :o_kr], pad_rope(kr), pad_rope(_swap_halves(kr)), ab,
                              w_in[:, o_qkv:o_z], w_in[:, o_z:o_a]], axis=1).astype(BF16)

    dq = MLA_NOPE + MLA_ROPE
    wq = w_q_up.reshape(Q_LORA, MLA_HEADS, dq)
    zpad = jnp.zeros((Q_LORA, MLA_HEADS, HEAD_BLOCK - dq), F32)
    wq_a = jnp.concatenate([wq, zpad], axis=-1)
    wq_b = jnp.concatenate([jnp.zeros((Q_LORA, MLA_HEADS, MLA_NOPE), F32),
                            _swap_halves(wq[..., MLA_NOPE:]), zpad], axis=-1)
    wq_p = jnp.concatenate([wq_a.reshape(Q_LORA, -1), wq_b.reshape(Q_LORA, -1)], axis=1).astype(BF16)

    wkv = w_kv_up.reshape(KV_LORA, MLA_HEADS, MLA_NOPE + MLA_V)
    wk = jnp.concatenate([wkv[..., :MLA_NOPE],
                          jnp.zeros((KV_LORA, MLA_HEADS, HEAD_BLOCK - MLA_NOPE), F32)], axis=-1)
    wk_p = wk.reshape(KV_LORA, -1).astype(BF16)
    wvt_p = wkv[..., MLA_NOPE:].reshape(KV_LORA, -1).T.astype(BF16)
    return w_in_p, wq_p, wk_p, wvt_p


def _rope_table(s):
    pos = jnp.arange(s, dtype=F32)
    inv = ROPE_THETA ** (-jnp.arange(0, MLA_ROPE, 2, dtype=F32) / MLA_ROPE)
    ang = pos[:, None] * inv[None, :]
    c, sn = jnp.cos(ang), jnp.sin(ang)
    cc = jnp.concatenate([c, c], axis=1)
    ss = jnp.concatenate([-sn, sn], axis=1)
    z_lo = jnp.zeros((s, MLA_NOPE), F32)
    z_hi = jnp.zeros((s, HEAD_BLOCK - MLA_NOPE - MLA_ROPE), F32)
    scale = (MLA_NOPE + MLA_ROPE) ** -0.5 * math.log2(math.e)
    ccq = jnp.concatenate([jnp.ones((s, MLA_NOPE), F32), cc, z_hi], axis=1) * scale
    ssq = jnp.concatenate([z_lo, ss, z_hi], axis=1) * scale
    cck = jnp.concatenate([z_lo, cc, z_hi], axis=1)
    ssk = jnp.concatenate([z_lo, ss, z_hi], axis=1)
    return jnp.concatenate([ccq, ssq, cck, ssk], axis=1)


def _lane_vec(v):
    v = v.reshape(1, -1).astype(F32)
    return jnp.concatenate([v, jnp.zeros((1, LANE - v.shape[1]), F32)], axis=1)


def _pick_tile(n, pref):
    t = min(pref, n)
    while n % t:
        t //= 2
    return t


def _trunk(x, layers, norm_final, tab):
    b, s, d = x.shape
    t = b * s
    tm = _pick_tile(s, 512)
    tq = _pick_tile(s, 512)
    kb = _pick_tile(s, 1024)
    h = x.reshape(t, d)
    depth = len(layers)
    for li, lw in enumerate(layers):
        lat, qkv, z = _in_proj(h, lw["norm_attn"], lw["w_in"], tm)
        q, k, v = _mla_proj(lat, tab, lw["q_norm"], lw["kv_norm"], lw["wq"], lw["wk"], lw["wvt"], b, s, tm)
        y_mla = _attention(q, k, v, tq, kb)
        o_gdn = _gdn(qkv, lat, lw["conv_w"], lw["a_log"], lw["dt_bias"], b, s)
        h = _out_ffn(h, y_mla, o_gdn, z, lw["gdn_norm"], lw["w_out"], lw["norm_ffn"], lw["w_gate"],
                     lw["w_up"], lw["w_down"], norm_final, tm, final=(li == depth - 1))
    return h.reshape(b, s, d)


def kernel(x_prompt, x_sample, norm_attn, w_in, q_norm, w_q_up, kv_norm, w_kv_up, conv_w, a_log, dt_bias, gdn_norm, w_out, norm_ffn, w_gate, w_up, w_down, norm_final):
    depth = w_in.shape[0]
    layers = []
    for l in range(depth):
        w_in_p, wq_p, wk_p, wvt_p = _prep_layer(w_in[l], w_q_up[l], w_kv_up[l])
        layers.append(dict(
            norm_attn=norm_attn[l].reshape(1, -1), w_in=w_in_p,
            q_norm=q_norm[l].reshape(1, -1), kv_norm=kv_norm[l].reshape(1, -1),
            wq=wq_p, wk=wk_p, wvt=wvt_p, conv_w=conv_w[l],
            a_log=_lane_vec(a_log[l]), dt_bias=_lane_vec(dt_bias[l]),
            gdn_norm=jnp.tile(gdn_norm[l].reshape(1, -1), (1, GDN_PAIR)),
            w_out=w_out[l].astype(BF16), norm_ffn=norm_ffn[l].reshape(1, -1),
            w_gate=w_gate[l].astype(BF16), w_up=w_up[l].astype(BF16),
            w_down=w_down[l].astype(BF16)))
    nfin = norm_final.reshape(1, -1)
    assert x_prompt.shape[1] == x_sample.shape[1]
    tab = _rope_table(x_prompt.shape[1])
    y_prompt = _trunk(x_prompt, layers, nfin, tab)
    y_sample = _trunk(x_sample, layers, nfin, tab)
    return (y_prompt, y_sample)
```
